```python
import math
import jax
import jax.numpy as jnp
from jax import lax
import numpy as np

D_MODEL = 2048
BATCH = 4
SEQ = 2048
DEPTH = 4
DEC_BATCH = 128
DEC_SEQ = 1
PAST_LEN = 16384
PAGE_SIZE = 128

F32 = jnp.float32
N_AB_LAYERS = (DEPTH + 1) // 2
N_C_LAYERS = DEPTH // 2
EPS = 1e-6
NEG_BIG = -1e30

D_FF = ((8 * D_MODEL // 3 + 127) // 128) * 128

S5_WIDTH = D_MODEL // 2
S5_GROUP = 16
S5_GROUPS = S5_WIDTH // S5_GROUP
S5_STATE = 64

HG_WIDTH = D_MODEL // 2
HG_EXPAND = 128
HG_HEADS = HG_WIDTH // HG_EXPAND
HG_DK = HG_EXPAND
HG_DV = HG_WIDTH // HG_HEADS
HG_KEY = HG_HEADS * HG_DK
HG_CHUNK = 32

AB_IN = S5_WIDTH + 2 * HG_KEY + 2 * HG_WIDTH
AB_OUT = S5_WIDTH + HG_WIDTH

ML_INNER = 2 * D_MODEL
ML_HEADS = 8
ML_DH = ML_INNER // ML_HEADS
ML_CONV = 4
ML_BLOCK = 4
ML_CHUNK = 64

kernel_name = 's5_hgrn2_mlstm_macaron_decode_step'


def rmsnorm(x, gain):
    xf = x.astype(F32)
    y = xf * lax.rsqrt(jnp.mean(xf * xf, axis=-1, keepdims=True) + EPS)
    return (y * gain.astype(F32)).astype(x.dtype)


def head_rmsnorm(x):
    return x * lax.rsqrt(jnp.mean(x * x, axis=-1, keepdims=True) + EPS)


def head_layernorm(x):
    mu = jnp.mean(x, axis=-1, keepdims=True)
    xc = x - mu
    return xc * lax.rsqrt(jnp.mean(xc * xc, axis=-1, keepdims=True) + EPS)


def swiglu(x, w_gate, w_up, w_down):
    return (jax.nn.silu(x @ w_gate) * (x @ w_up)) @ w_down


def pad_time(a, pad, value):
    if pad == 0:
        return a
    cfg = [(0, 0)] * a.ndim
    cfg[1] = (0, pad)
    return jnp.pad(a, cfg, constant_values=value)


def to_chunks(a, n, L):
    b, h = a.shape[0], a.shape[2]
    a = a.reshape((b, n, L, h) + a.shape[3:])
    return jnp.swapaxes(jnp.moveaxis(a, 1, 0), 2, 3)


def from_chunks(a):
    n, b, h, L, d = a.shape
    return jnp.moveaxis(jnp.swapaxes(a, 2, 3), 0, 1).reshape(b, n * L, h, d)


def _affine_combine(e1, e2):
    a1r, a1i, b1r, b1i = e1
    a2r, a2i, b2r, b2i = e2
    return (a2r * a1r - a2i * a1i, a2r * a1i + a2i * a1r,
            a2r * b1r - a2i * b1i + b2r, a2r * b1i + a2i * b1r + b2i)


def s5_ssm(u, h0_re, h0_im, lam_re, lam_im, log_step, b_re, b_im, c_re, c_im, d_skip, glu_w, glu_b):
    bsz, t, _ = u.shape
    ug = u.reshape(bsz, t, S5_GROUPS, S5_GROUP)
    lr = jnp.minimum(lam_re.astype(F32), -1e-4)
    li = lam_im.astype(F32)
    step = jnp.exp(log_step.astype(F32))[:, None]
    mag = jnp.exp(lr * step)
    ang = li * step
    a_re = mag * jnp.cos(ang)
    a_im = mag * jnp.sin(ang)
    den = lr * lr + li * li
    nr = a_re - 1.0
    g_re = (nr * lr + a_im * li) / den
    g_im = (a_im * lr - nr * li) / den
    br = b_re.astype(F32)
    bi = b_im.astype(F32)
    bb_re = g_re[..., None] * br - g_im[..., None] * bi
    bb_im = g_re[..., None] * bi + g_im[..., None] * br
    bu_re = jnp.einsum('btgc,gpc->btgp', ug, bb_re)
    bu_im = jnp.einsum('btgc,gpc->btgp', ug, bb_im)
    init_re = a_re * h0_re - a_im * h0_im
    init_im = a_re * h0_im + a_im * h0_re
    bu_re = bu_re.at[:, 0].add(init_re)
    bu_im = bu_im.at[:, 0].add(init_im)
    ar = jnp.broadcast_to(a_re, bu_re.shape)
    ai = jnp.broadcast_to(a_im, bu_im.shape)
    _, _, xr, xi = lax.associative_scan(_affine_combine, (ar, ai, bu_re, bu_im), axis=1)
    y = (jnp.einsum('btgp,gcp->btgc', xr, c_re.astype(F32))
         - jnp.einsum('btgp,gcp->btgc', xi, c_im.astype(F32))
         + d_skip.astype(F32).reshape(S5_GROUPS, S5_GROUP) * ug)
    y = jax.nn.gelu(y.reshape(bsz, t, S5_WIDTH))
    out = y * jax.nn.sigmoid(y @ glu_w.astype(F32) + glu_b.astype(F32))
    return out, xr[:, -1], xi[:, -1]


def hgrn2_recurrence(q, k, v, logf, s0):
    t = q.shape[1]
    L = min(HG_CHUNK, t)
    n = -(-t // L)
    pad = n * L - t
    q, k, v, logf = (pad_time(a, pad, 0.0) for a in (q, k, v, logf))
    causal = jnp.tril(jnp.ones((L, L), dtype=bool))[:, :, None]

    def chunk_step(S, xs):
        qc, kc, vc, fc = xs
        b = jnp.cumsum(fc, axis=2)
        diff = jnp.where(causal, b[:, :, :, None, :] - b[:, :, None, :, :], -jnp.inf)
        scores = jnp.einsum('bhtk,bhsk,bhtsk->bhts', qc, kc, jnp.exp(diff))
        o = (jnp.einsum('bhts,bhsv->bhtv', scores, vc)
             + jnp.einsum('bhtk,bhkv->bhtv', qc * jnp.exp(b), S))
        b_last = b[:, :, -1:, :]
        S_new = (jnp.exp(b_last[:, :, 0, :])[..., None] * S
                 + jnp.einsum('bhsk,bhsv->bhkv', kc * jnp.exp(b_last - b), vc))
        return S_new, o

    S, o = lax.scan(chunk_step, s0, tuple(to_chunks(a, n, L) for a in (q, k, v, logf)))
    return from_chunks(o)[:, :t], S


def mlstm_recurrence(q, k, v, ig, lf, c0, n0, m0):
    t = q.shape[1]
    L = min(ML_CHUNK, t)
    n = -(-t // L)
    pad = n * L - t
    q, k, v, lf = (pad_time(a, pad, 0.0) for a in (q, k, v, lf))
    ig = pad_time(ig, pad, NEG_BIG)
    causal = jnp.tril(jnp.ones((L, L), dtype=bool))

    def chunk_step(carry, xs):
        C, nv, m = carry
        qc, kc, vc, ic, fc = xs
        b = jnp.cumsum(fc, axis=-1)
        dlog = jnp.where(causal, b[..., :, None] - b[..., None, :] + ic[..., None, :], -jnp.inf)
        g = b + m[..., None]
        mt = jnp.maximum(g, jnp.max(dlog, axis=-1))
        w = jnp.exp(dlog - mt[..., None])
        gi = jnp.exp(g - mt)
        qk = jnp.einsum('bhtd,bhsd->bhts', qc, kc) * w
        num = (jnp.einsum('bhts,bhsv->bhtv', qk, vc)
               + gi[..., None] * jnp.einsum('bhtk,bhkv->bhtv', qc, C))
        den = jnp.sum(qk, axis=-1) + gi * jnp.einsum('bhtk,bhk->bht', qc, nv)
        h = num / jnp.maximum(jnp.abs(den), jnp.exp(-mt))[..., None]
        m_new = mt[..., -1]
        carry_decay = jnp.exp(b[..., -1] + m - m_new)
        ws = jnp.exp(b[..., -1:] - b + ic - m_new[..., None])
        C_new = carry_decay[..., None, None] * C + jnp.einsum('bhs,bhsk,bhsv->bhkv', ws, kc, vc)
        n_new = carry_decay[..., None] * nv + jnp.einsum('bhs,bhsk->bhk', ws, kc)
        return (C_new, n_new, m_new), h

    xs = tuple(to_chunks(a, n, L) for a in (q, k, v, ig, lf))
    (C, nv, m), h = lax.scan(chunk_step, (c0, n0, m0), xs)
    return from_chunks(h)[:, :t], C, nv, m


def hgrn_lower_bounds(logits):
    sm = jax.nn.softmax(logits.astype(F32), axis=0)
    cum = jnp.cumsum(sm, axis=0)
    return cum - cum[0:1]


def ab_mixer(x, h0_re, h0_im, s0, lb, p, j):
    bsz, t = x.shape[:2]
    proj = x @ p['ab_w_in'][j]
    o1 = S5_WIDTH
    o2 = o1 + HG_KEY
    o3 = o2 + HG_KEY
    o4 = o3 + HG_WIDTH
    y_a, s5_re, s5_im = s5_ssm(proj[..., :o1].astype(F32), h0_re.astype(F32), h0_im.astype(F32),
                               p['s5_lambda_re'][j], p['s5_lambda_im'][j], p['s5_log_step'][j],
                               p['s5_b_re'][j], p['s5_b_im'][j], p['s5_c_re'][j], p['s5_c_im'][j],
                               p['s5_d'][j], p['s5_glu_w'][j], p['s5_glu_b'][j])
    q = jax.nn.silu(proj[..., o1:o2].astype(F32)).reshape(bsz, t, HG_HEADS, HG_DK)
    fl = proj[..., o2:o3].astype(F32).reshape(bsz, t, HG_HEADS, HG_DK)
    lbh = lb.reshape(HG_HEADS, HG_DK)
    logf = jnp.logaddexp(jax.nn.log_sigmoid(fl), jnp.log(lbh) + jax.nn.log_sigmoid(-fl))
    k = (1.0 - lbh) * jax.nn.sigmoid(-fl)
    v = proj[..., o3:o4].astype(F32).reshape(bsz, t, HG_HEADS, HG_DV)
    gate = jax.nn.silu(proj[..., o4:].astype(F32)).reshape(bsz, t, HG_HEADS, HG_DV)
    o, s_new = hgrn2_recurrence(q, k, v, logf, s0.astype(F32))
    gain = p['hgrn_norm_gain'][j].astype(F32).reshape(HG_HEADS, HG_DV)
    y_b = (head_rmsnorm(o) * gain * gate).reshape(bsz, t, HG_WIDTH)
    mix = jnp.concatenate([y_a, y_b], axis=-1).astype(x.dtype) @ p['ab_w_out'][j]
    return mix, s5_re, s5_im, s_new


def mlstm_mixer(x, conv_buf, c0, n0, m0, p, j):
    bsz, t = x.shape[:2]
    up = x @ p['ml_w_up'][j]
    xm = up[..., :ML_INNER]
    z = up[..., ML_INNER:]
    xpad = jnp.concatenate([conv_buf.astype(xm.dtype), xm], axis=1)
    conv_w = p['ml_conv_w'][j]
    xc = p['ml_conv_b'][j]
    for tap in range(ML_CONV):
        xc = xc + xpad[:, tap:tap + t] * conv_w[tap]
    xc = jax.nn.silu(xc)
    new_buf = xpad[:, t:]

    def headwise(a, w):
        a4 = a.reshape(bsz, t, ML_INNER // ML_BLOCK, ML_BLOCK)
        return jnp.einsum('btnc,ncd->btnd', a4, w).reshape(bsz, t, ML_INNER)

    q = headwise(xc, p['ml_w_q'][j])
    k = headwise(xc, p['ml_w_k'][j])
    v = headwise(xm, p['ml_w_v'][j])
    gates = (jnp.concatenate([q, k, v], axis=-1) @ p['ml_w_gates'][j] + p['ml_b_gates'][j]).astype(F32)
    ig = gates[..., :ML_HEADS]
    lf = jax.nn.log_sigmoid(gates[..., ML_HEADS:])
    qh = q.astype(F32).reshape(bsz, t, ML_HEADS, ML_DH)
    kh = k.astype(F32).reshape(bsz, t, ML_HEADS, ML_DH) * (ML_DH ** -0.5)
    vh = v.astype(F32).reshape(bsz, t, ML_HEADS, ML_DH)
    h, C, nv, m = mlstm_recurrence(qh, kh, vh, ig, lf, c0.astype(F32), n0.astype(F32), m0.astype(F32))
    h = head_layernorm(h) * p['ml_norm_gain'][j].astype(F32).reshape(ML_HEADS, ML_DH)
    h = h.reshape(bsz, t, ML_INNER) + p['ml_skip'][j].astype(F32) * xc.astype(F32)
    h = h * jax.nn.silu(z.astype(F32))
    out = h.astype(x.dtype) @ p['ml_w_down'][j]
    return out, new_buf, C, nv, m


def run_trunk(x, s5_re, s5_im, hg_state, conv_state, c_state, n_state, m_state, p):
    lower_bounds = hgrn_lower_bounds(p['hgrn_lb_logits'])
    o_s5r, o_s5i, o_hg, o_conv, o_c, o_n, o_m = [], [], [], [], [], [], []
    h = x
    for layer in range(DEPTH):
        ng = p['norm_gain'][layer]
        f_a = swiglu(rmsnorm(h, ng[0]), p['ffn_w_gate'][layer, 0], p['ffn_w_up'][layer, 0], p['ffn_w_down'][layer, 0])
        h = h + 0.5 * rmsnorm(f_a, ng[1])
        hn = rmsnorm(h, ng[2])
        j = layer // 2
        if layer % 2 == 0:
            mix, sr, si, sh = ab_mixer(hn, s5_re[j], s5_im[j], hg_state[j], lower_bounds[j], p, j)
            o_s5r.append(sr)
            o_s5i.append(si)
            o_hg.append(sh)
        else:
            mix, cb, cc, cn, cm = mlstm_mixer(hn, conv_state[j], c_state[j], n_state[j], m_state[j], p, j)
            o_conv.append(cb)
            o_c.append(cc)
            o_n.append(cn)
            o_m.append(cm)
        h = h + rmsnorm(mix, ng[3])
        f_b = swiglu(rmsnorm(h, ng[4]), p['ffn_w_gate'][layer, 1], p['ffn_w_up'][layer, 1], p['ffn_w_down'][layer, 1])
        h = h + 0.5 * rmsnorm(f_b, ng[5])
    dt = x.dtype
    return (h, jnp.stack(o_s5r).astype(dt), jnp.stack(o_s5i).astype(dt), jnp.stack(o_hg).astype(dt),
            jnp.stack(o_conv).astype(dt), jnp.stack(o_c).astype(dt), jnp.stack(o_n).astype(dt),
            jnp.stack(o_m).astype(dt))


def setup_inputs(seed: int = 0) -> dict:
    key = jax.random.key(seed)
    ks = iter(jax.random.split(key, 48))

    def nrm(shape, scale):
        return scale * jax.random.normal(next(ks), shape, F32)

    def gain(shape):
        return 1.0 + nrm(shape, 0.05)

    return {
        'x_prompt': nrm((BATCH, SEQ, D_MODEL), 1.0),
        'x_sample': nrm((DEC_BATCH, DEC_SEQ, D_MODEL), 1.0),
        'state_s5_re': nrm((N_AB_LAYERS, DEC_BATCH, S5_GROUPS, S5_STATE), 0.5),
        'state_s5_im': nrm((N_AB_LAYERS, DEC_BATCH, S5_GROUPS, S5_STATE), 0.5),
        'state_hgrn': nrm((N_AB_LAYERS, DEC_BATCH, HG_HEADS, HG_DK, HG_DV), 0.5),
        'state_conv': nrm((N_C_LAYERS, DEC_BATCH, ML_CONV - 1, ML_INNER), 1.0),
        'state_mlstm_c': nrm((N_C_LAYERS, DEC_BATCH, ML_HEADS, ML_DH, ML_DH), 0.5 * ML_DH ** -0.5),
        'state_mlstm_n': nrm((N_C_LAYERS, DEC_BATCH, ML_HEADS, ML_DH), ML_DH ** -0.5),
        'state_mlstm_m': jax.random.uniform(next(ks), (N_C_LAYERS, DEC_BATCH, ML_HEADS), F32, -1.0, 2.0),
        'norm_gain': gain((DEPTH, 6, D_MODEL)),
        'ffn_w_gate': nrm((DEPTH, 2, D_MODEL, D_FF), D_MODEL ** -0.5),
        'ffn_w_up': nrm((DEPTH, 2, D_MODEL, D_FF), D_MODEL ** -0.5),
        'ffn_w_down': nrm((DEPTH, 2, D_FF, D_MODEL), D_FF ** -0.5),
        'ab_w_in': nrm((N_AB_LAYERS, D_MODEL, AB_IN), D_MODEL ** -0.5),
        'ab_w_out': nrm((N_AB_LAYERS, AB_OUT, D_MODEL), AB_OUT ** -0.5),
        's5_lambda_re': -0.5 + nrm((N_AB_LAYERS, S5_GROUPS, S5_STATE), 0.01),
        's5_lambda_im': jnp.pi * jnp.arange(S5_STATE, dtype=F32) + nrm((N_AB_LAYERS, S5_GROUPS, S5_STATE), 0.01),
        's5_log_step': jax.random.uniform(next(ks), (N_AB_LAYERS, S5_GROUPS), F32, math.log(1e-3), math.log(1e-1)),
        's5_b_re': nrm((N_AB_LAYERS, S5_GROUPS, S5_STATE, S5_GROUP), (2 * S5_GROUP) ** -0.5),
        's5_b_im': nrm((N_AB_LAYERS, S5_GROUPS, S5_STATE, S5_GROUP), (2 * S5_GROUP) ** -0.5),
        's5_c_re': nrm((N_AB_LAYERS, S5_GROUPS, S5_GROUP, S5_STATE), S5_STATE ** -0.5),
        's5_c_im': nrm((N_AB_LAYERS, S5_GROUPS, S5_GROUP, S5_STATE), S5_STATE ** -0.5),
        's5_d': nrm((N_AB_LAYERS, S5_WIDTH), 1.0),
        's5_glu_w': nrm((N_AB_LAYERS, S5_WIDTH, S5_WIDTH), S5_WIDTH ** -0.5),
        's5_glu_b': nrm((N_AB_LAYERS, S5_WIDTH), 0.01),
        'hgrn_lb_logits': nrm((N_AB_LAYERS, HG_KEY), 1.0),
        'hgrn_norm_gain': gain((N_AB_LAYERS, HG_WIDTH)),
        'ml_w_up': nrm((N_C_LAYERS, D_MODEL, 2 * ML_INNER), D_MODEL ** -0.5),
        'ml_conv_w': nrm((N_C_LAYERS, ML_CONV, ML_INNER), ML_CONV ** -0.5),
        'ml_conv_b': nrm((N_C_LAYERS, ML_INNER), 0.01),
        'ml_w_q': nrm((N_C_LAYERS, ML_INNER // ML_BLOCK, ML_BLOCK, ML_BLOCK), ML_BLOCK ** -0.5),
        'ml_w_k': nrm((N_C_LAYERS, ML_INNER // ML_BLOCK, ML_BLOCK, ML_BLOCK), ML_BLOCK ** -0.5),
        'ml_w_v': nrm((N_C_LAYERS, ML_INNER // ML_BLOCK, ML_BLOCK, ML_BLOCK), ML_BLOCK ** -0.5),
        'ml_w_gates': nrm((N_C_LAYERS, 3 * ML_INNER, 2 * ML_HEADS), (3 * ML_INNER) ** -0.5),
        'ml_b_gates': jnp.concatenate([nrm((N_C_LAYERS, ML_HEADS), 0.1),
                                       jnp.linspace(3.0, 6.0, ML_HEADS, dtype=F32)[None] + nrm((N_C_LAYERS, ML_HEADS), 0.01)],
                                      axis=-1),
        'ml_norm_gain': gain((N_C_LAYERS, ML_INNER)),
        'ml_skip': gain((N_C_LAYERS, ML_INNER)),
        'ml_w_down': nrm((N_C_LAYERS, ML_INNER, D_MODEL), ML_INNER ** -0.5),
    }


def reference(x_prompt, x_sample, state_s5_re, state_s5_im, state_hgrn, state_conv, state_mlstm_c,
              state_mlstm_n, state_mlstm_m, norm_gain, ffn_w_gate, ffn_w_up, ffn_w_down, ab_w_in, ab_w_out,
              s5_lambda_re, s5_lambda_im, s5_log_step, s5_b_re, s5_b_im, s5_c_re, s5_c_im, s5_d, s5_glu_w,
              s5_glu_b, hgrn_lb_logits, hgrn_norm_gain, ml_w_up, ml_conv_w, ml_conv_b, ml_w_q, ml_w_k, ml_w_v,
              ml_w_gates, ml_b_gates, ml_norm_gain, ml_skip, ml_w_down):
    p = {
        'norm_gain': norm_gain, 'ffn_w_gate': ffn_w_gate, 'ffn_w_up': ffn_w_up, 'ffn_w_down': ffn_w_down,
        'ab_w_in': ab_w_in, 'ab_w_out': ab_w_out, 's5_lambda_re': s5_lambda_re, 's5_lambda_im': s5_lambda_im,
        's5_log_step': s5_log_step, 's5_b_re': s5_b_re, 's5_b_im': s5_b_im, 's5_c_re': s5_c_re,
        's5_c_im': s5_c_im, 's5_d': s5_d, 's5_glu_w': s5_glu_w, 's5_glu_b': s5_glu_b,
        'hgrn_lb_logits': hgrn_lb_logits, 'hgrn_norm_gain': hgrn_norm_gain, 'ml_w_up': ml_w_up,
        'ml_conv_w': ml_conv_w, 'ml_conv_b': ml_conv_b, 'ml_w_q': ml_w_q, 'ml_w_k': ml_w_k, 'ml_w_v': ml_w_v,
        'ml_w_gates': ml_w_gates, 'ml_b_gates': ml_b_gates, 'ml_norm_gain': ml_norm_gain, 'ml_skip': ml_skip,
        'ml_w_down': ml_w_down,
    }
    bp = x_prompt.shape[0]
    dt = x_prompt.dtype
    y_prompt, p_s5_re, p_s5_im, p_hgrn, p_conv, p_c, p_n, p_m = run_trunk(
        x_prompt,
        jnp.zeros((N_AB_LAYERS, bp, S5_GROUPS, S5_STATE), dt),
        jnp.zeros((N_AB_LAYERS, bp, S5_GROUPS, S5_STATE), dt),
        jnp.zeros((N_AB_LAYERS, bp, HG_HEADS, HG_DK, HG_DV), dt),
        jnp.zeros((N_C_LAYERS, bp, ML_CONV - 1, ML_INNER), dt),
        jnp.zeros((N_C_LAYERS, bp, ML_HEADS, ML_DH, ML_DH), dt),
        jnp.zeros((N_C_LAYERS, bp, ML_HEADS, ML_DH), dt),
        jnp.zeros((N_C_LAYERS, bp, ML_HEADS), dt),
        p)
    y_sample, s_s5_re, s_s5_im, s_hgrn, s_conv, s_c, s_n, s_m = run_trunk(
        x_sample, state_s5_re, state_s5_im, state_hgrn, state_conv, state_mlstm_c, state_mlstm_n,
        state_mlstm_m, p)
    return (y_prompt, y_sample, p_s5_re, p_s5_im, p_hgrn, p_conv, p_c, p_n, p_m,
            s_s5_re, s_s5_im, s_hgrn, s_conv, s_c, s_n, s_m)
```

```python
import functools
import math

import jax
import jax.numpy as jnp
from jax import lax
from jax.experimental import pallas as pl
from jax.experimental.pallas import tpu as pltpu

F32 = jnp.float32
BF16 = jnp.bfloat16
EPS = 1e-6
MIB = 1024 * 1024

S5_GROUP = 16
S5_STATE = 64
HG_HEADS = 8
ML_HEADS = 8
ML_CONV = 4
ML_BLOCK = 4

LANE = 128
MXU_DIM = 256


def _dot(a, b):
    return jnp.dot(a.astype(BF16), b.astype(BF16), preferred_element_type=F32)


def _dot_nt(a, b):
    return lax.dot_general(a.astype(BF16), b.astype(BF16), (((1,), (1,)), ((), ())),
                           preferred_element_type=F32)


def _dot_tn(a, b):
    return lax.dot_general(a.astype(BF16), b.astype(BF16), (((0,), (0,)), ((), ())),
                           preferred_element_type=F32)


def _sigmoid(x):
    return 1.0 / (1.0 + jnp.exp(-x))


def _silu(x):
    return x * _sigmoid(x)


def _log_sigmoid(x):
    return jnp.minimum(x, 0.0) - jnp.log1p(jnp.exp(-jnp.abs(x)))


def _logaddexp(a, b):
    return jnp.maximum(a, b) + jnp.log1p(jnp.exp(-jnp.abs(a - b)))


def _gelu_tanh(x):
    return x * (0.5 * (1.0 + jnp.tanh(math.sqrt(2.0 / math.pi) * (x + 0.044715 * (x * x * x)))))


def _rms(x, g):
    return x * lax.rsqrt(jnp.mean(x * x, axis=-1, keepdims=True) + EPS) * g


def _shift_rows(x, s):
    n = x.shape[0]
    if s % 8 == 0:
        return jnp.concatenate([jnp.zeros((s,) + x.shape[1:], x.dtype), x[:n - s]], axis=0)
    row = lax.broadcasted_iota(jnp.int32, x.shape, 0)
    return jnp.where(row >= s, pltpu.roll(x, s, 0), 0.0)


def _params(n_axes, vmem_mib):
    return pltpu.CompilerParams(dimension_semantics=("arbitrary",) * n_axes,
                                vmem_limit_bytes=vmem_mib * MIB)


def _ffn_body(has_tail, *refs):
    if has_tail:
        (x_ref, gpre_ref, gpost_ref, wg_ref, wu_ref, wd_ref, wgt_ref, wut_ref, wdt_ref, o_ref, xn_ref) = refs
    else:
        (x_ref, gpre_ref, gpost_ref, wg_ref, wu_ref, wd_ref, o_ref, xn_ref) = refs
    j = pl.program_id(1)
    last = pl.num_programs(1) - 1

    @pl.when(j == 0)
    def _():
        xn_ref[...] = _rms(x_ref[...], gpre_ref[...]).astype(BF16)

    def part(wg, wu, wd):
        xn = xn_ref[...]
        g = _dot(xn, wg)
        u = _dot(xn, wu)
        return _dot(_silu(g) * u, wd)

    c = part(wg_ref[...], wu_ref[...], wd_ref[...])

    @pl.when(j == 0)
    def _():
        o_ref[...] = c

    @pl.when(j > 0)
    def _():
        o_ref[...] += c

    @pl.when(j == last)
    def _():
        acc = o_ref[...]
        if has_tail:
            acc = acc + part(wgt_ref[...], wut_ref[...], wdt_ref[...])
        o_ref[...] = x_ref[...] + 0.5 * _rms(acc, gpost_ref[...])


def _ffn(h, gains, wg, wu, wd, layer, idx, k_pre, k_post, tm, tf):
    m, d = h.shape
    dff = wg.shape[-1]
    nj = dff // tf
    tail = dff - nj * tf
    assert m % tm == 0 and tail in (0, LANE) and nj >= 1
    tail_blk = dff // LANE - 1
    one = pl.Buffered(1)
    in_specs = [
        pl.BlockSpec((tm, d), lambda i, j: (i, 0), pipeline_mode=one),
        pl.BlockSpec((None, None, 1, d), lambda i, j: (layer, k_pre, 0, 0)),
        pl.BlockSpec((None, None, 1, d), lambda i, j: (layer, k_post, 0, 0)),
        pl.BlockSpec((None, None, d, tf), lambda i, j: (layer, idx, 0, j)),
        pl.BlockSpec((None, None, d, tf), lambda i, j: (layer, idx, 0, j)),
        pl.BlockSpec((None, None, tf, d), lambda i, j: (layer, idx, j, 0)),
    ]
    args = [h, gains, gains, wg, wu, wd]
    if tail:
        in_specs += [
            pl.BlockSpec((None, None, d, LANE), lambda i, j: (layer, idx, 0, tail_blk), pipeline_mode=one),
            pl.BlockSpec((None, None, d, LANE), lambda i, j: (layer, idx, 0, tail_blk), pipeline_mode=one),
            pl.BlockSpec((None, None, LANE, d), lambda i, j: (layer, idx, tail_blk, 0), pipeline_mode=one),
        ]
        args += [wg, wu, wd]
    return pl.pallas_call(
        functools.partial(_ffn_body, bool(tail)),
        grid=(m // tm, nj),
        in_specs=in_specs,
        out_specs=pl.BlockSpec((tm, d), lambda i, j: (i, 0)),
        out_shape=jax.ShapeDtypeStruct((m, d), F32),
        scratch_shapes=[pltpu.VMEM((tm, d), BF16)],
        compiler_params=_params(2, 58),
        name="ffn",
    )(*args)


def _nmm_body(x_ref, g_ref, w_ref, o_ref, xn_ref):
    @pl.when(pl.program_id(1) == 0)
    def _():
        xn_ref[...] = _rms(x_ref[...], g_ref[...]).astype(BF16)

    o_ref[...] = _dot(xn_ref[...], w_ref[...])


def _norm_matmul(h, gains, layer, k_gain, w, j_w, tm, tn):
    m, d = h.shape
    n = w.shape[-1]
    assert m % tm == 0 and n % tn == 0
    return pl.pallas_call(
        _nmm_body,
        grid=(m // tm, n // tn),
        in_specs=[
            pl.BlockSpec((tm, d), lambda i, j: (i, 0), pipeline_mode=pl.Buffered(1)),
            pl.BlockSpec((None, None, 1, d), lambda i, j: (layer, k_gain, 0, 0)),
            pl.BlockSpec((None, d, tn), lambda i, j: (j_w, 0, j)),
        ],
        out_specs=pl.BlockSpec((tm, tn), lambda i, j: (i, j)),
        out_shape=jax.ShapeDtypeStruct((m, n), F32),
        scratch_shapes=[pltpu.VMEM((tm, d), BF16)],
        compiler_params=_params(2, 48),
        name="norm_matmul",
    )(h, gains, w)


def _mmpn_body(steps, *refs):
    n_in = len(steps)
    y_refs = refs[:n_in]
    w_ref, h_ref, g_ref, o_ref = refs[n_in:]
    k = pl.program_id(1)
    last = pl.num_programs(1) - 1

    @pl.when(k == 0)
    def _():
        o_ref[...] = jnp.zeros_like(o_ref)

    s0 = 0
    for y_ref, ns in zip(y_refs, steps):
        @pl.when((k >= s0) & (k < s0 + ns))
        def _(y_ref=y_ref):
            o_ref[...] += _dot(y_ref[...], w_ref[...])
        s0 += ns

    @pl.when(k == last)
    def _():
        o_ref[...] = h_ref[...] + _rms(o_ref[...], g_ref[...])


def _matmul_postnorm(ys, w, j_w, h, gains, layer, k_gain, tm, tk):
    m, d = h.shape
    steps = tuple(y.shape[1] // tk for y in ys)
    assert all(y.shape[1] % tk == 0 for y in ys) and sum(steps) * tk == w.shape[1] and m % tm == 0
    in_specs = []
    s0 = 0
    for ns in steps:
        in_specs.append(pl.BlockSpec(
            (tm, tk), lambda i, k, s0=s0, ns=ns: (i, jnp.clip(k - s0, 0, ns - 1))))
        s0 += ns
    in_specs += [
        pl.BlockSpec((None, tk, d), lambda i, k: (j_w, k, 0)),
        pl.BlockSpec((tm, d), lambda i, k: (i, 0)),
        pl.BlockSpec((None, None, 1, d), lambda i, k: (layer, k_gain, 0, 0)),
    ]
    return pl.pallas_call(
        functools.partial(_mmpn_body, steps),
        grid=(m // tm, sum(steps)),
        in_specs=in_specs,
        out_specs=pl.BlockSpec((tm, d), lambda i, k: (i, 0)),
        out_shape=jax.ShapeDtypeStruct((m, d), F32),
        compiler_params=_params(2, 48),
        name="matmul_postnorm",
    )(*ys, w, h, gains)


def _s5_param_body(tt, lre_ref, lim_ref, ls_ref, br_ref, bi_ref,
                   ap2r_ref, ap2i_ref, tabr_ref, tabi_ref, bbr_ref, bbi_ref):
    lr = jnp.minimum(lre_ref[...], -1e-4)
    li = lim_ref[...]
    step = jnp.exp(ls_ref[...])
    mag = jnp.exp(lr * step)
    ang = li * step
    ar = mag * jnp.cos(ang)
    ai = mag * jnp.sin(ang)
    den = lr * lr + li * li
    nr = ar - 1.0
    gr = (nr * lr + ai * li) / den
    gi = (ai * lr - nr * li) / den
    br = br_ref[...]
    bi = bi_ref[...]
    bbr_ref[...] = gr * br - gi * bi
    bbi_ref[...] = gr * bi + gi * br
    pr, pi = ar, ai
    tabr_ref[0:1, :] = ar
    tabi_ref[0:1, :] = ai
    for k in range(ap2r_ref.shape[0]):
        ap2r_ref[k:k + 1, :] = pr
        ap2i_ref[k:k + 1, :] = pi
        n = 2 ** k
        if n < tt:
            tr = tabr_ref[0:n, :]
            ti = tabi_ref[0:n, :]
            tabr_ref[n:2 * n, :] = tr * pr - ti * pi
            tabi_ref[n:2 * n, :] = tr * pi + ti * pr
        pr, pi = pr * pr - pi * pi, pr * pi + pi * pr


def _s5_params(lam_re, lam_im, log_step, b_re, b_im, tt):
    n = lam_re.shape[-1]
    c = b_re.shape[0]
    full = lambda shape: pl.BlockSpec(shape, lambda: (0,) * len(shape))
    return pl.pallas_call(
        functools.partial(_s5_param_body, tt),
        in_specs=[full((1, n))] * 3 + [full((c, n))] * 2,
        out_specs=[full((8, n))] * 2 + [full((tt, n))] * 2 + [full((c, n))] * 2,
        out_shape=[jax.ShapeDtypeStruct((8, n), F32)] * 2 + [jax.ShapeDtypeStruct((tt, n), F32)] * 2
        + [jax.ShapeDtypeStruct((c, n), F32)] * 2,
        compiler_params=pltpu.CompilerParams(vmem_limit_bytes=32 * MIB),
        name="s5_params",
    )(lam_re, lam_im, log_step, b_re, b_im)


def _s5_body(scan, n_chunks, *refs):
    if scan:
        (u_ref, ap2r_ref, ap2i_ref, tabr_ref, tabi_ref, bbr_ref, bbi_ref, ccr_ref, cci_ref, d_ref,
         gw_ref, gb_ref, y_ref, sr_ref, si_ref, cr_s, ci_s) = refs
    else:
        (u_ref, ap2r_ref, ap2i_ref, bbr_ref, bbi_ref, ccr_ref, cci_ref, d_ref,
         gw_ref, gb_ref, h0r_ref, h0i_ref, y_ref, sr_ref, si_ref) = refs
    tt = u_ref.shape[0]
    cw = u_ref.shape[1] // n_chunks
    sw = bbr_ref.shape[2]

    if scan:
        @pl.when(pl.program_id(1) == 0)
        def _():
            cr_s[...] = jnp.zeros_like(cr_s)
            ci_s[...] = jnp.zeros_like(ci_s)

    u = u_ref[...]
    ys = []
    for c in range(n_chunks):
        uc = u[:, c * cw:(c + 1) * cw]
        lanes = slice(c * sw, (c + 1) * sw)
        xr = _dot(uc, bbr_ref[c])
        xi = _dot(uc, bbi_ref[c])
        if scan:
            s = 1
            k = 0
            while s < tt:
                pr = ap2r_ref[k:k + 1, lanes]
                pi = ap2i_ref[k:k + 1, lanes]
                sr = _shift_rows(xr, s)
                si = _shift_rows(xi, s)
                xr, xi = xr + (pr * sr - pi * si), xi + (pr * si + pi * sr)
                s *= 2
                k += 1
            tr = tabr_ref[:, lanes]
            ti = tabi_ref[:, lanes]
            hr = cr_s[:, lanes]
            hi = ci_s[:, lanes]
        else:
            tr = ap2r_ref[0:1, lanes]
            ti = ap2i_ref[0:1, lanes]
            hr = h0r_ref[:, lanes]
            hi = h0i_ref[:, lanes]
        xr, xi = xr + (tr * hr - ti * hi), xi + (tr * hi + ti * hr)
        if scan:
            cr_s[:, lanes] = xr[tt - 1:tt]
            ci_s[:, lanes] = xi[tt - 1:tt]
        else:
            sr_ref[:, lanes] = xr
            si_ref[:, lanes] = xi
        ys.append(_dot(xr, ccr_ref[c]) - _dot(xi, cci_ref[c]) + d_ref[:, c * cw:(c + 1) * cw] * uc)
    y = _gelu_tanh(jnp.concatenate(ys, axis=-1))
    y_ref[...] = y * _sigmoid(_dot(y, gw_ref[...]) + gb_ref[...])
    if scan:
        sr_ref[0] = cr_s[...]
        si_ref[0] = ci_s[...]


def _s5_specs(j, width, n_chunks, cw, sw, n_state, nidx):
    zero = (0,) * nidx
    return dict(
        ap2=pl.BlockSpec((8, n_state), lambda *a: (0, 0)),
        bb=pl.BlockSpec((n_chunks, cw, sw), lambda *a: (0, 0, 0)),
        cc=pl.BlockSpec((n_chunks, sw, cw), lambda *a: (0, 0, 0)),
        d=pl.BlockSpec((None, 1, width), lambda *a: (j, 0, 0)),
        gw=pl.BlockSpec((None, width, width), lambda *a: (j, 0, 0)),
        gb=pl.BlockSpec((None, 1, width), lambda *a: (j, 0, 0)),
    )


def _s5_seq(proj, row0, bsz, t, tt, j, prm, d3, glu_w, glu_b3):
    ap2r, ap2i, tabr, tabi, bbr, bbi, ccr, cci = prm
    n_chunks, cw, sw = bbr.shape
    width = n_chunks * cw
    n_state = n_chunks * sw
    nt = t // tt
    rb0 = row0 // tt
    sp = _s5_specs(j, width, n_chunks, cw, sw, n_state, 2)
    tab = pl.BlockSpec((tt, n_state), lambda b, i: (0, 0))
    st = pl.BlockSpec((1, 1, n_state), lambda b, i: (b, 0, 0))
    y, sr, si = pl.pallas_call(
        functools.partial(_s5_body, True, n_chunks),
        grid=(bsz, nt),
        in_specs=[pl.BlockSpec((tt, width), lambda b, i: (rb0 + b * nt + i, 0)),
                  sp["ap2"], sp["ap2"], tab, tab, sp["bb"], sp["bb"], sp["cc"], sp["cc"],
                  sp["d"], sp["gw"], sp["gb"]],
        out_specs=[pl.BlockSpec((tt, width), lambda b, i: (b * nt + i, 0)), st, st],
        out_shape=[jax.ShapeDtypeStruct((bsz * t, width), F32),
                   jax.ShapeDtypeStruct((bsz, 1, n_state), F32),
                   jax.ShapeDtypeStruct((bsz, 1, n_state), F32)],
        scratch_shapes=[pltpu.VMEM((1, n_state), F32)] * 2,
        compiler_params=_params(2, 48),
        name="s5_seq",
    )(proj, ap2r, ap2i, tabr, tabi, bbr, bbi, ccr, cci, d3, glu_w, glu_b3)
    return y, sr[:, 0], si[:, 0]


def _s5_step(proj, row0, bsz, j, prm, d3, glu_w, glu_b3, h0r, h0i):
    ap2r, ap2i, _, _, bbr, bbi, ccr, cci = prm
    n_chunks, cw, sw = bbr.shape
    width = n_chunks * cw
    n_state = n_chunks * sw
    rb0 = row0 // bsz
    sp = _s5_specs(j, width, n_chunks, cw, sw, n_state, 1)
    st = pl.BlockSpec((bsz, n_state), lambda i: (0, 0))
    return pl.pallas_call(
        functools.partial(_s5_body, False, n_chunks),
        grid=(1,),
        in_specs=[pl.BlockSpec((bsz, width), lambda i: (rb0, 0)),
                  sp["ap2"], sp["ap2"], sp["bb"], sp["bb"], sp["cc"], sp["cc"],
                  sp["d"], sp["gw"], sp["gb"], st, st],
        out_specs=[pl.BlockSpec((bsz, width), lambda i: (0, 0)), st, st],
        out_shape=[jax.ShapeDtypeStruct((bsz, width), F32),
                   jax.ShapeDtypeStruct((bsz, n_state), F32),
                   jax.ShapeDtypeStruct((bsz, n_state), F32)],
        compiler_params=_params(1, 48),
        name="s5_step",
    )(proj, ap2r, ap2i, bbr, bbi, ccr, cci, d3, glu_w, glu_b3, h0r, h0i)


def _hgrn_lower_bound(logits, j, axis):
    n = logits.shape[axis]
    mx = jnp.max(logits, axis=axis, keepdims=True)
    e = jnp.exp(logits - mx)
    sm = e / jnp.sum(e, axis=axis, keepdims=True)
    pick = (lambda i: sm[i:i + 1, :]) if axis == 0 else (lambda i: sm[:, i:i + 1])
    cum = pick(0)
    for i in range(1, j + 1):
        cum = cum + pick(i)
    assert j < n
    return cum - pick(0)


def _hgrn_gates(fl, lb):
    logf = _logaddexp(_log_sigmoid(fl), jnp.log(lb) + _log_sigmoid(-fl))
    k = (1.0 - lb) * _sigmoid(-fl)
    return logf, k


def _hgrn_seq_body(j, chunk, sub, q_ref, f_ref, v_ref, gt_ref, lg_ref, gain_ref,
                   y_ref, s_ref, st_s, q_s, k_s, v_s, b_s, o_s):
    tt, dk = q_ref.shape
    i = pl.program_id(2)

    @pl.when(i == 0)
    def _():
        st_s[...] = jnp.zeros_like(st_s)

    lb = _hgrn_lower_bound(lg_ref[...], j, 0)
    logf, k = _hgrn_gates(f_ref[...], lb)
    rin = lax.broadcasted_iota(jnp.int32, (tt, dk), 0) & (chunk - 1)
    b = logf
    s = 1
    while s < chunk:
        b = b + jnp.where(rin >= s, _shift_rows(b, s), 0.0)
        s *= 2
    q_s[...] = _silu(q_ref[...])
    k_s[...] = k
    v_s[...] = v_ref[...]
    b_s[...] = b
    nsub = chunk // sub
    tsub = lax.broadcasted_iota(jnp.int32, (sub, dk), 0)

    def one_chunk(c, carry):
        r0 = pl.multiple_of(c * chunk, chunk)
        qc = q_s[pl.ds(r0, chunk), :]
        kc = k_s[pl.ds(r0, chunk), :]
        vc = v_s[pl.ds(r0, chunk), :]
        bc = b_s[pl.ds(r0, chunk), :]
        acc = [jnp.zeros((sub, dk), F32) for _ in range(nsub)]
        for s_i in range(chunk):
            ks = kc[s_i:s_i + 1]
            bs = bc[s_i:s_i + 1]
            vs = vc[s_i:s_i + 1]
            sb = s_i // sub
            for tb in range(sb, nsub):
                diff = bc[tb * sub:(tb + 1) * sub] - bs
                if tb == sb:
                    diff = jnp.where(tsub >= s_i - sb * sub, diff, -jnp.inf)
                p = qc[tb * sub:(tb + 1) * sub] * ks * jnp.exp(diff)
                acc[tb] = acc[tb] + jnp.sum(p, axis=-1, keepdims=True) * vs
        st = st_s[...]
        o = jnp.concatenate(acc, axis=0) + _dot_nt(qc * jnp.exp(bc), st)
        b_last = bc[chunk - 1:chunk]
        st_s[...] = st * jnp.exp(b_last) + _dot_tn(vc, kc * jnp.exp(b_last - bc))
        o_s[pl.ds(r0, chunk), :] = o
        return carry

    lax.fori_loop(0, tt // chunk, one_chunk, 0)
    o = o_s[...]
    o = o * lax.rsqrt(jnp.mean(o * o, axis=-1, keepdims=True) + EPS)
    y_ref[...] = o * gain_ref[...] * _silu(gt_ref[...])

    @pl.when(i == pl.num_programs(2) - 1)
    def _():
        s_ref[...] = st_s[...].T


def _hgrn_seq(proj, row0, bsz, t, tt, j, col0, logits, gain3, chunk=32, sub=8):
    h, dk = HG_HEADS, LANE
    nt = t // tt
    rb0 = row0 // tt
    cb = col0 // dk
    sec = lambda n: pl.BlockSpec((tt, dk), lambda b, hh, i: (rb0 + b * nt + i, cb + n * h + hh))
    y, s = pl.pallas_call(
        functools.partial(_hgrn_seq_body, j, chunk, sub),
        grid=(bsz, h, nt),
        in_specs=[sec(0), sec(1), sec(2), sec(3),
                  pl.BlockSpec((logits.shape[0], dk), lambda b, hh, i: (0, hh)),
                  pl.BlockSpec((None, 1, dk), lambda b, hh, i: (j, 0, hh))],
        out_specs=[pl.BlockSpec((tt, dk), lambda b, hh, i: (b * nt + i, hh)),
                   pl.BlockSpec((None, None, dk, dk), lambda b, hh, i: (b, hh, 0, 0))],
        out_shape=[jax.ShapeDtypeStruct((bsz * t, h * dk), F32),
                   jax.ShapeDtypeStruct((bsz, h, dk, dk), F32)],
        scratch_shapes=[pltpu.VMEM((dk, dk), F32)] + [pltpu.VMEM((tt, dk), F32)] * 5,
        compiler_params=_params(3, 32),
        name="hgrn_seq",
    )(proj, proj, proj, proj, logits, gain3)
    return y, s


def _hgrn_step_body(j, bb, q_ref, f_ref, v_ref, gt_ref, qc_ref, fc_ref, lg_ref, lgt_ref, gain_ref,
                    s_ref, y_ref, so_ref):
    lb_r = _hgrn_lower_bound(lg_ref[...], j, 0)
    lb_c = _hgrn_lower_bound(lgt_ref[...], j, 1)
    qcol = _silu(qc_ref[...])
    logf_c, kcol = _hgrn_gates(fc_ref[...], lb_c)
    fcol = jnp.exp(logf_c)
    for n in range(bb):
        s = s_ref[n]
        q_r = _silu(q_ref[n])
        _, k_r = _hgrn_gates(f_ref[n], lb_r)
        v_r = v_ref[n]
        q_c = qcol[:, n:n + 1]
        f_c = fcol[:, n:n + 1]
        k_c = kcol[:, n:n + 1]
        score = jnp.sum(q_r * k_r, axis=-1, keepdims=True)
        o = score * v_r + jnp.sum((q_c * f_c) * s, axis=0, keepdims=True)
        so_ref[n] = f_c * s + k_c * v_r
        o = o * lax.rsqrt(jnp.mean(o * o, axis=-1, keepdims=True) + EPS)
        y_ref[n] = o * gain_ref[...] * _silu(gt_ref[n])


def _hgrn_step(proj3, proj_t, j, col0, logits, logits_t, gain3, s_all, bb=8):
    bsz = proj3.shape[0]
    h, dk = HG_HEADS, LANE
    cb = col0 // dk
    qt, ft = proj_t
    row = lambda n: pl.BlockSpec((bb, 1, dk), lambda hh, i: (i, 0, cb + n * h + hh))
    col = pl.BlockSpec((None, None, dk, bb), lambda hh, i: (hh, i, 0, 0))
    st = pl.BlockSpec((bb, None, dk, dk), lambda hh, i: (i, hh, 0, 0))
    st_in = pl.BlockSpec((None, bb, None, dk, dk), lambda hh, i: (j, i, hh, 0, 0))
    y, s = pl.pallas_call(
        functools.partial(_hgrn_step_body, j, bb),
        grid=(h, bsz // bb),
        in_specs=[row(0), row(1), row(2), row(3), col, col,
                  pl.BlockSpec((logits.shape[0], dk), lambda hh, i: (0, hh)),
                  pl.BlockSpec((dk, logits.shape[0]), lambda hh, i: (hh, 0)),
                  pl.BlockSpec((None, 1, dk), lambda hh, i: (j, 0, hh)),
                  st_in],
        out_specs=[pl.BlockSpec((bb, 1, dk), lambda hh, i: (i, 0, hh)), st],
        out_shape=[jax.ShapeDtypeStruct((bsz, 1, h * dk), F32),
                   jax.ShapeDtypeStruct(s_all.shape[1:], F32)],
        compiler_params=_params(2, 32),
        name="hgrn_step",
    )(proj3, proj3, proj3, proj3, qt, ft, logits, logits_t, gain3, s_all)
    return y.reshape(bsz, h * dk), s


def _ml_qkv_gates(c, xc, xm, bdq_ref, bdk_ref, bdv_ref, wg_ref, bg_ref, q_ref, k_ref, v_ref, xc_ref, g_ref):
    nblk = bdq_ref.shape[0]
    w = bdq_ref.shape[1]
    xcb = xc.astype(BF16)
    xmb = xm.astype(BF16)
    q = jnp.concatenate([_dot(xcb[:, s * w:(s + 1) * w], bdq_ref[s]) for s in range(nblk)], axis=-1)
    k = jnp.concatenate([_dot(xcb[:, s * w:(s + 1) * w], bdk_ref[s]) for s in range(nblk)], axis=-1)
    v = jnp.concatenate([_dot(xmb[:, s * w:(s + 1) * w], bdv_ref[s]) for s in range(nblk)], axis=-1)
    q_ref[...] = q
    k_ref[...] = k
    v_ref[...] = v
    xc_ref[...] = xc
    g = _dot(q, wg_ref[0]) + _dot(k, wg_ref[1]) + _dot(v, wg_ref[2])

    @pl.when(c == 0)
    def _():
        g_ref[...] = g + bg_ref[...]

    @pl.when(c > 0)
    def _():
        g_ref[...] += g


def _ml_pre_seq_body(xm_ref, cw_ref, cb_ref, bdq_ref, bdk_ref, bdv_ref, wg_ref, bg_ref,
                     q_ref, k_ref, v_ref, xc_ref, g_ref, stage_s, halo_s):
    i = pl.program_id(1)
    c = pl.program_id(2)
    tt = xm_ref.shape[0]

    @pl.when(i == 0)
    def _():
        halo_s[c] = jnp.zeros(halo_s.shape[1:], F32)

    xm = xm_ref[...]
    stage_s[0:8, :] = halo_s[c]
    stage_s[8:, :] = xm
    halo_s[c] = xm[tt - 8:tt]
    acc = cb_ref[...] + cw_ref[ML_CONV - 1:ML_CONV, :] * xm
    for d in range(1, ML_CONV):
        acc = acc + cw_ref[ML_CONV - 1 - d:ML_CONV - d, :] * stage_s[8 - d:8 - d + tt, :]
    _ml_qkv_gates(c, _silu(acc), xm, bdq_ref, bdk_ref, bdv_ref, wg_ref, bg_ref,
                  q_ref, k_ref, v_ref, xc_ref, g_ref)


def _ml_pre_step_body(xm_ref, taps_ref, cw_ref, cb_ref, bdq_ref, bdk_ref, bdv_ref, wg_ref, bg_ref,
                      q_ref, k_ref, v_ref, xc_ref, g_ref):
    c = pl.program_id(0)
    xm = xm_ref[...]
    acc = cb_ref[...] + cw_ref[ML_CONV - 1:ML_CONV, :] * xm
    for d in range(1, ML_CONV):
        acc = acc + cw_ref[ML_CONV - 1 - d:ML_CONV - d, :] * taps_ref[ML_CONV - 1 - d]
    _ml_qkv_gates(c, _silu(acc), xm, bdq_ref, bdk_ref, bdv_ref, wg_ref, bg_ref,
                  q_ref, k_ref, v_ref, xc_ref, g_ref)


def _ml_pre(up, row0, bsz, t, tt, lc, j, conv_w, conv_b3, bd, wg4, bg3, taps=None):
    inner = conv_w.shape[-1]
    ng = wg4.shape[-1]
    nc = inner // lc
    nblk = lc // MXU_DIM
    bdq, bdk, bdv = bd
    seq = taps is None
    rows = bsz * t
    if seq:
        nt = t // tt
        rb0 = row0 // tt
        grid = (bsz, nt, nc)
        rmap = lambda b, i, c: b * nt + i
        cmap = lambda b, i, c: c
    else:
        tt = bsz
        rb0 = row0 // tt
        grid = (nc,)
        rmap = lambda c: 0
        cmap = lambda c: c
    in_specs = [pl.BlockSpec((tt, lc), lambda *a: (rb0 + rmap(*a), cmap(*a)))]
    args = [up]
    if not seq:
        in_specs.append(pl.BlockSpec((ML_CONV - 1, tt, lc), lambda *a: (0, 0, cmap(*a))))
        args.append(taps)
    in_specs += [
        pl.BlockSpec((None, ML_CONV, lc), lambda *a: (j, 0, cmap(*a))),
        pl.BlockSpec((None, 1, lc), lambda *a: (j, 0, cmap(*a))),
        pl.BlockSpec((None, nblk, MXU_DIM, MXU_DIM), lambda *a: (j, cmap(*a), 0, 0)),
        pl.BlockSpec((None, nblk, MXU_DIM, MXU_DIM), lambda *a: (j, cmap(*a), 0, 0)),
        pl.BlockSpec((None, nblk, MXU_DIM, MXU_DIM), lambda *a: (j, cmap(*a), 0, 0)),
        pl.BlockSpec((None, 3, lc, ng), lambda *a: (j, 0, cmap(*a), 0)),
        pl.BlockSpec((None, 1, ng), lambda *a: (j, 0, 0)),
    ]
    args += [conv_w, conv_b3, bdq, bdk, bdv, wg4, bg3]
    big = pl.BlockSpec((tt, lc), lambda *a: (rmap(*a), cmap(*a)))
    return pl.pallas_call(
        _ml_pre_seq_body if seq else _ml_pre_step_body,
        grid=grid,
        in_specs=in_specs,
        out_specs=[big, big, big, big, pl.BlockSpec((tt, ng), lambda *a: (rmap(*a), 0))],
        out_shape=[jax.ShapeDtypeStruct((rows, inner), F32)] * 4 + [jax.ShapeDtypeStruct((rows, ng), F32)],
        scratch_shapes=([pltpu.VMEM((tt + 8, lc), F32), pltpu.VMEM((nc, 8, lc), F32)] if seq else []),
        compiler_params=_params(len(grid), 48),
        name="ml_pre_seq" if seq else "ml_pre_step",
    )(*args)


def _ml_out(h, gain, skip, xc, z):
    mu = jnp.mean(h, axis=-1, keepdims=True)
    hc = h - mu
    hn = hc * lax.rsqrt(jnp.mean(hc * hc, axis=-1, keepdims=True) + EPS)
    return (hn * gain + skip * xc) * _silu(z)


def _ml_seq_body(q_ref, k_ref, v_ref, xc_ref, z_ref, g_ref, gt_ref, gain_ref, skip_ref,
                 o_ref, c_ref, n_ref, m_ref):
    hh = pl.program_id(1)
    i = pl.program_id(2)
    L, dh = q_ref.shape
    nh = g_ref.shape[1] // 2

    @pl.when(i == 0)
    def _():
        c_ref[...] = jnp.zeros_like(c_ref)
        n_ref[...] = jnp.zeros_like(n_ref)
        m_ref[...] = jnp.zeros_like(m_ref)

    g = g_ref[...]
    gt = gt_ref[...]
    lane = lax.broadcasted_iota(jnp.int32, g.shape, 1)
    sub = lax.broadcasted_iota(jnp.int32, gt.shape, 0)
    ig_c = jnp.sum(jnp.where(lane == hh, g, 0.0), axis=1, keepdims=True)
    lf_c = _log_sigmoid(jnp.sum(jnp.where(lane == nh + hh, g, 0.0), axis=1, keepdims=True))
    ig_r = jnp.sum(jnp.where(sub == hh, gt, 0.0), axis=0, keepdims=True)
    lf_r = _log_sigmoid(jnp.sum(jnp.where(sub == nh + hh, gt, 0.0), axis=0, keepdims=True))
    tr = lax.broadcasted_iota(jnp.int32, (L, L), 0)
    tc = lax.broadcasted_iota(jnp.int32, (L, L), 1)
    causal = tr >= tc
    b_c = jnp.sum(jnp.where(causal, lf_r, 0.0), axis=1, keepdims=True)
    b_r = jnp.sum(jnp.where(tr <= tc, lf_c, 0.0), axis=0, keepdims=True)
    m = m_ref[...]
    dlog = jnp.where(causal, b_c - b_r + ig_r, -jnp.inf)
    gsum = b_c + m
    mt = jnp.maximum(gsum, jnp.max(dlog, axis=1, keepdims=True))
    w = jnp.exp(dlog - mt)
    gi = jnp.exp(gsum - mt)
    q = q_ref[...]
    k = k_ref[...] * (dh ** -0.5)
    v = v_ref[...]
    c_old = c_ref[...]
    n_old = n_ref[...]
    qk = _dot_nt(q, k) * w
    num = _dot(qk, v) + gi * _dot(q, c_old)
    den = jnp.sum(qk, axis=1, keepdims=True) + gi * jnp.sum(q * n_old, axis=1, keepdims=True)
    h = num / jnp.maximum(jnp.abs(den), jnp.exp(-mt))
    m_new = mt[L - 1:L]
    b_last = b_c[L - 1:L]
    decay = jnp.exp(b_last + m - m_new)
    ks = k * jnp.exp(b_last - b_c + ig_c - m_new)
    c_ref[...] = decay * c_old + _dot_tn(ks, v)
    n_ref[...] = decay * n_old + jnp.sum(ks, axis=0, keepdims=True)
    m_ref[...] = m_new
    o_ref[...] = _ml_out(h, gain_ref[...], skip_ref[...], xc_ref[...], z_ref[...])


def _ml_seq(q, k, v, xc, up, row0, gates, gates_t, bsz, t, L, j, gain3, skip3):
    inner = q.shape[1]
    nh = ML_HEADS
    dh = inner // nh
    nt = t // L
    rb0 = row0 // L
    blk = pl.BlockSpec((L, dh), lambda b, hh, i: (b * nt + i, hh))
    par = pl.BlockSpec((None, 1, dh), lambda b, hh, i: (j, 0, hh))
    return pl.pallas_call(
        _ml_seq_body,
        grid=(bsz, nh, nt),
        in_specs=[blk, blk, blk, blk,
                  pl.BlockSpec((L, dh), lambda b, hh, i: (rb0 + b * nt + i, nh + hh)),
                  pl.BlockSpec((L, 2 * nh), lambda b, hh, i: (b * nt + i, 0)),
                  pl.BlockSpec((2 * nh, L), lambda b, hh, i: (0, b * nt + i)),
                  par, par],
        out_specs=[blk,
                   pl.BlockSpec((None, None, dh, dh), lambda b, hh, i: (b, hh, 0, 0)),
                   pl.BlockSpec((None, None, 1, dh), lambda b, hh, i: (b, hh, 0, 0)),
                   pl.BlockSpec((None, None, 1, 1), lambda b, hh, i: (b, hh, 0, 0))],
        out_shape=[jax.ShapeDtypeStruct((bsz * t, inner), F32),
                   jax.ShapeDtypeStruct((bsz, nh, dh, dh), F32),
                   jax.ShapeDtypeStruct((bsz, nh, 1, dh), F32),
                   jax.ShapeDtypeStruct((bsz, nh, 1, 1), F32)],
        compiler_params=_params(3, 48),
        name="ml_seq",
    )(q, k, v, xc, up, gates, gates_t, gain3, skip3)


def _ml_step_body(bb, q_ref, k_ref, v_ref, xc_ref, z_ref, qc_ref, kc_ref, ig_ref, lf_ref, m_ref,
                  gain_ref, skip_ref, c_ref, n_ref, o_ref, co_ref, no_ref, mo_ref):
    dh = q_ref.shape[-1]
    scale = dh ** -0.5
    ig_all = ig_ref[...]
    lf_all = _log_sigmoid(lf_ref[...])
    m_all = m_ref[...]
    g_all = lf_all + m_all
    mt_all = jnp.maximum(g_all, ig_all)
    w_all = jnp.exp(ig_all - mt_all)
    gi_all = jnp.exp(g_all - mt_all)
    mo_ref[...] = mt_all
    for n in range(bb):
        c_old = c_ref[n]
        n_old = n_ref[n]
        q_r = q_ref[n]
        k_r = k_ref[n] * scale
        v_r = v_ref[n]
        q_c = qc_ref[:, n:n + 1]
        k_c = kc_ref[:, n:n + 1] * scale
        mt = mt_all[:, n:n + 1]
        w = w_all[:, n:n + 1]
        gi = gi_all[:, n:n + 1]
        qk = jnp.sum(q_r * k_r, axis=-1, keepdims=True) * w
        num = qk * v_r + gi * jnp.sum(q_c * c_old, axis=0, keepdims=True)
        den = qk + gi * jnp.sum(q_r * n_old, axis=-1, keepdims=True)
        h = num / jnp.maximum(jnp.abs(den), jnp.exp(-mt))
        co_ref[n] = gi * c_old + (w * k_c) * v_r
        no_ref[n] = gi * n_old + w * k_r
        o_ref[n] = _ml_out(h, gain_ref[...], skip_ref[...], xc_ref[n], z_ref[n])


def _ml_step(q3, k3, v3, xc3, up3, q_t, k_t, ig_t, lf_t, m_t, j, gain3, skip3, c_all, n_all, bb=4):
    bsz, _, inner = q3.shape
    nh = ML_HEADS
    dh = inner // nh
    row = pl.BlockSpec((bb, 1, dh), lambda hh, i: (i, 0, hh))
    col = pl.BlockSpec((None, None, dh, bb), lambda hh, i: (hh, i, 0, 0))
    sc = pl.BlockSpec((None, None, 1, bb), lambda hh, i: (hh, i, 0, 0))
    par = pl.BlockSpec((None, 1, dh), lambda hh, i: (j, 0, hh))
    cst = pl.BlockSpec((bb, None, dh, dh), lambda hh, i: (i, hh, 0, 0))
    nst = pl.BlockSpec((bb, None, 1, dh), lambda hh, i: (i, hh, 0, 0))
    cst_in = pl.BlockSpec((None, bb, None, dh, dh), lambda hh, i: (j, i, hh, 0, 0))
    nst_in = pl.BlockSpec((None, bb, None, 1, dh), lambda hh, i: (j, i, hh, 0, 0))
    return pl.pallas_call(
        functools.partial(_ml_step_body, bb),
        grid=(nh, bsz // bb),
        in_specs=[row, row, row, row,
                  pl.BlockSpec((bb, 1, dh), lambda hh, i: (i, 0, nh + hh)),
                  col, col, sc, sc, sc, par, par, cst_in, nst_in],
        out_specs=[row, cst, nst, sc],
        out_shape=[jax.ShapeDtypeStruct((bsz, 1, inner), F32),
                   jax.ShapeDtypeStruct(c_all.shape[1:], F32),
                   jax.ShapeDtypeStruct(n_all.shape[1:], F32),
                   jax.ShapeDtypeStruct(m_t.shape, F32)],
        compiler_params=_params(2, 48),
        name="ml_step",
    )(q3, k3, v3, xc3, up3, q_t, k_t, ig_t, lf_t, m_t, gain3, skip3, c_all, n_all)


def _block_diag(w, size):
    n, c, _ = w.shape
    per = size // c
    w4 = w.reshape(n // per, per, c, c)
    eye = jnp.eye(per, dtype=w.dtype)
    out = w4[:, :, :, None, :] * eye[None, :, None, :, None]
    return out.reshape(n // per, size, size).astype(BF16)


def _to_cols(x, nh, bb):
    bsz, w = x.shape
    d = w // nh
    return x.reshape(bsz // bb, bb, nh, d).transpose(2, 0, 3, 1)


def _s5_layout(bbt_r, bbt_i, c_re, c_im, n_chunks):
    ch, n_state = bbt_r.shape
    g = c_re.shape[0]
    p = n_state // g
    per = g // n_chunks
    eye = jnp.eye(per, dtype=F32)

    def bmat(bt):
        b4 = bt.reshape(ch, n_chunks, per, p).transpose(1, 0, 2, 3)
        out = eye[None, :, None, :, None] * b4[:, None, :, :, :]
        return out.reshape(n_chunks, per * ch, per * p).astype(BF16)

    def cmat(c):
        c4 = c.reshape(n_chunks, per, ch, p).transpose(0, 1, 3, 2)
        out = c4[:, :, :, None, :] * eye[None, :, None, :, None]
        return out.reshape(n_chunks, per * p, per * ch).astype(BF16)

    return bmat(bbt_r), bmat(bbt_i), cmat(c_re), cmat(c_im)


TM_FFN = 832
TF_FFN = 256
TM_PROJ = 1040
TN_PROJ = 512
TM_OUT = 520
TK_OUT = 512
TT_S5 = 128
TT_HGRN = 256
TT_MLPRE = 512
LC_MLPRE = 512
L_ML = 256
S5_CHUNKS = 8


def kernel(x_prompt, x_sample, state_s5_re, state_s5_im, state_hgrn, state_conv, state_mlstm_c, state_mlstm_n, state_mlstm_m, norm_gain, ffn_w_gate, ffn_w_up, ffn_w_down, ab_w_in, ab_w_out, s5_lambda_re, s5_lambda_im, s5_log_step, s5_b_re, s5_b_im, s5_c_re, s5_c_im, s5_d, s5_glu_w, s5_glu_b, hgrn_lb_logits, hgrn_norm_gain, ml_w_up, ml_conv_w, ml_conv_b, ml_w_q, ml_w_k, ml_w_v, ml_w_gates, ml_b_gates, ml_norm_gain, ml_skip, ml_w_down):
    bp, t, d = x_prompt.shape
    bs = x_sample.shape[0]
    depth = norm_gain.shape[0]
    mp = bp * t
    n_ab, g5, p5 = s5_lambda_re.shape
    n_state = g5 * p5
    s5_w = g5 * S5_GROUP
    hg_w = HG_HEADS * LANE
    inner = ml_conv_w.shape[-1]
    n_c = ml_conv_w.shape[0]
    dh = inner // ML_HEADS

    gains = norm_gain.reshape(depth, norm_gain.shape[1], 1, d)
    h = jnp.concatenate([x_prompt.reshape(mp, d), x_sample.reshape(bs, d)], axis=0)

    out = dict(p_s5r=[], p_s5i=[], p_hg=[], p_conv=[], p_c=[], p_n=[], p_m=[],
               s_s5r=[], s_s5i=[], s_hg=[], s_conv=[], s_c=[], s_n=[], s_m=[])

    for layer in range(depth):
        j = layer // 2
        h = _ffn(h, gains, ffn_w_gate, ffn_w_up, ffn_w_down, layer, 0, 0, 1, TM_FFN, TF_FFN)
        if layer % 2 == 0:
            proj = _norm_matmul(h, gains, layer, 2, ab_w_in, j, TM_PROJ, TN_PROJ)
            prm = _s5_params(s5_lambda_re[j].reshape(1, n_state), s5_lambda_im[j].reshape(1, n_state),
                             jnp.repeat(s5_log_step[j], p5).reshape(1, n_state),
                             s5_b_re[j].transpose(2, 0, 1).reshape(S5_GROUP, n_state),
                             s5_b_im[j].transpose(2, 0, 1).reshape(S5_GROUP, n_state), TT_S5)
            ap2r, ap2i, tabr, tabi, bbt_r, bbt_i = prm
            bbr, bbi, ccr, cci = _s5_layout(bbt_r, bbt_i, s5_c_re[j], s5_c_im[j], S5_CHUNKS)
            s5p = (ap2r, ap2i, tabr, tabi, bbr, bbi, ccr, cci)
            d3 = s5_d.reshape(n_ab, 1, s5_w)
            gb3 = s5_glu_b.reshape(n_ab, 1, s5_w)
            ya_p, sr_p, si_p = _s5_seq(proj, 0, bp, t, TT_S5, j, s5p, d3, s5_glu_w, gb3)
            ya_s, sr_s, si_s = _s5_step(proj, mp, bs, j, s5p, d3, s5_glu_w, gb3,
                                        state_s5_re[j].reshape(bs, n_state), state_s5_im[j].reshape(bs, n_state))
            out["p_s5r"].append(sr_p.reshape(bp, g5, p5))
            out["p_s5i"].append(si_p.reshape(bp, g5, p5))
            out["s_s5r"].append(sr_s.reshape(bs, g5, p5))
            out["s_s5i"].append(si_s.reshape(bs, g5, p5))
            hgain3 = hgrn_norm_gain.reshape(n_ab, 1, hg_w)
            yb_p, hg_p = _hgrn_seq(proj, 0, bp, t, TT_HGRN, j, s5_w, hgrn_lb_logits, hgain3)
            proj_s = proj[mp:]
            bb_h = 8
            yb_s, hg_s = _hgrn_step(
                proj_s.reshape(bs, 1, proj.shape[1]),
                (_to_cols(proj_s[:, s5_w:s5_w + hg_w], HG_HEADS, bb_h),
                 _to_cols(proj_s[:, s5_w + hg_w:s5_w + 2 * hg_w], HG_HEADS, bb_h)),
                j, s5_w, hgrn_lb_logits, hgrn_lb_logits.T, hgain3, state_hgrn, bb_h)
            out["p_hg"].append(hg_p)
            out["s_hg"].append(hg_s)
            ya = jnp.concatenate([ya_p, ya_s], axis=0)
            yb = jnp.concatenate([yb_p, yb_s], axis=0)
            h = _matmul_postnorm([ya, yb], ab_w_out, j, h, gains, layer, 3, TM_OUT, TK_OUT)
        else:
            up = _norm_matmul(h, gains, layer, 2, ml_w_up, j, TM_PROJ, TN_PROJ)
            bd = (_block_diag(ml_w_q[j], MXU_DIM)[None], _block_diag(ml_w_k[j], MXU_DIM)[None],
                  _block_diag(ml_w_v[j], MXU_DIM)[None])
            wg4 = ml_w_gates[j].reshape(1, 3, inner, 2 * ML_HEADS)
            bg3 = ml_b_gates[j].reshape(1, 1, 2 * ML_HEADS)
            cw = ml_conv_w[j][None]
            cb3 = ml_conv_b[j].reshape(1, 1, inner)
            gain3 = ml_norm_gain.reshape(n_c, 1, inner)
            skip3 = ml_skip.reshape(n_c, 1, inner)
            q_p, k_p, v_p, xc_p, g_p = _ml_pre(up, 0, bp, t, TT_MLPRE, LC_MLPRE, 0, cw, cb3, bd, wg4, bg3)
            hf_p, c_p, n_p, m_p = _ml_seq(q_p, k_p, v_p, xc_p, up, 0, g_p, g_p.T, bp, t, L_ML, j, gain3, skip3)
            out["p_conv"].append(up[:mp, :inner].reshape(bp, t, inner)[:, t - (ML_CONV - 1):])
            out["p_c"].append(c_p)
            out["p_n"].append(n_p.reshape(bp, ML_HEADS, dh))
            out["p_m"].append(m_p.reshape(bp, ML_HEADS))
            taps = state_conv[j].transpose(1, 0, 2)
            q_s, k_s, v_s, xc_s, g_s = _ml_pre(up, mp, bs, 1, None, LC_MLPRE, 0, cw, cb3, bd, wg4, bg3, taps=taps)
            bb_m = 4
            sc = lambda a: a.T.reshape(ML_HEADS, bs // bb_m, 1, bb_m)
            r3 = lambda a: a.reshape(bs, 1, a.shape[-1])
            hf_s, c_s, n_s, m_s = _ml_step(
                r3(q_s), r3(k_s), r3(v_s), r3(xc_s), r3(up[mp:]),
                _to_cols(q_s, ML_HEADS, bb_m), _to_cols(k_s, ML_HEADS, bb_m),
                sc(g_s[:, :ML_HEADS]), sc(g_s[:, ML_HEADS:]), sc(state_mlstm_m[j]),
                j, gain3, skip3, state_mlstm_c, state_mlstm_n.reshape(n_c, bs, ML_HEADS, 1, dh), bb_m)
            out["s_conv"].append(jnp.concatenate([state_conv[j][:, 1:], up[mp:, None, :inner]], axis=1))
            out["s_c"].append(c_s)
            out["s_n"].append(n_s.reshape(bs, ML_HEADS, dh))
            out["s_m"].append(m_s.reshape(ML_HEADS, bs).T)
            hf = jnp.concatenate([hf_p, hf_s.reshape(bs, inner)], axis=0)
            h = _matmul_postnorm([hf], ml_w_down, j, h, gains, layer, 3, TM_OUT, TK_OUT)
        h = _ffn(h, gains, ffn_w_gate, ffn_w_up, ffn_w_down, layer, 1, 4, 5, TM_FFN, TF_FFN)

    st = lambda name: jnp.stack(out[name])
    return (h[:mp].reshape(bp, t, d), h[mp:].reshape(bs, 1, d),
            st("p_s5r"), st("p_s5i"), st("p_hg"), st("p_conv"), st("p_c"), st("p_n"), st("p_m"),
            st("s_s5r"), st("s_s5i"), st("s_hg"), st("s_conv"), st("s_c"), st("s_n"), st("s_m"))
```

```python
import functools
import math

import jax
import jax.numpy as jnp
from jax import lax
from jax.experimental import pallas as pl
from jax.experimental.pallas import tpu as pltpu

F32 = jnp.float32
BF16 = jnp.bfloat16
EPS = 1e-6
MIB = 1024 * 1024

S5_GROUP = 16
S5_STATE = 64
HG_HEADS = 8
ML_HEADS = 8
ML_CONV = 4
ML_BLOCK = 4

LANE = 128
MXU_DIM = 256


def _dot(a, b):
    return jnp.dot(a.astype(BF16), b.astype(BF16), preferred_element_type=F32)


def _dot_nt(a, b):
    return lax.dot_general(a.astype(BF16), b.astype(BF16), (((1,), (1,)), ((), ())),
                           preferred_element_type=F32)


def _dot_tn(a, b):
    return lax.dot_general(a.astype(BF16), b.astype(BF16), (((0,), (0,)), ((), ())),
                           preferred_element_type=F32)


def _sigmoid(x):
    return 1.0 / (1.0 + jnp.exp(-x))


def _silu(x):
    return x * _sigmoid(x)


def _log_sigmoid(x):
    return jnp.minimum(x, 0.0) - jnp.log1p(jnp.exp(-jnp.abs(x)))


def _logaddexp(a, b):
    return jnp.maximum(a, b) + jnp.log1p(jnp.exp(-jnp.abs(a - b)))


def _gelu_tanh(x):
    return x * (0.5 * (1.0 + jnp.tanh(math.sqrt(2.0 / math.pi) * (x + 0.044715 * (x * x * x)))))


def _rms(x, g):
    return x * lax.rsqrt(jnp.mean(x * x, axis=-1, keepdims=True) + EPS) * g


def _shift_rows(x, s):
    n = x.shape[0]
    if s % 8 == 0:
        return jnp.concatenate([jnp.zeros((s,) + x.shape[1:], x.dtype), x[:n - s]], axis=0)
    row = lax.broadcasted_iota(jnp.int32, x.shape, 0)
    return jnp.where(row >= s, pltpu.roll(x, s, 0), 0.0)


def _shift_rows_in_groups(x, s):
    row = lax.broadcasted_iota(jnp.int32, x.shape, 0)
    return jnp.where((row & 7) >= s, pltpu.roll(x, s, 0), 0.0)


def _params(n_axes, vmem_mib):
    return pltpu.CompilerParams(dimension_semantics=("arbitrary",) * n_axes,
                                vmem_limit_bytes=vmem_mib * MIB)


def _ffn_body(has_tail, *refs):
    if has_tail:
        (x_ref, gpre_ref, gpost_ref, wg_ref, wu_ref, wd_ref, wgt_ref, wut_ref, wdt_ref,
         o_ref, xn_ref, h_ref) = refs
    else:
        (x_ref, gpre_ref, gpost_ref, wg_ref, wu_ref, wd_ref, o_ref, xn_ref, h_ref) = refs
    j = pl.program_id(1)
    nj = pl.num_programs(1) - 1

    @pl.when(j == 0)
    def _():
        xn_ref[...] = _rms(x_ref[...], gpre_ref[...]).astype(BF16)
        o_ref[...] = jnp.zeros_like(o_ref)
        h_ref[...] = jnp.zeros_like(h_ref)

    def hidden(wg, wu):
        xn = xn_ref[...]
        return (_silu(_dot(xn, wg)) * _dot(xn, wu)).astype(BF16)

    @pl.when(j < nj)
    def _():
        o_ref[...] += _dot(h_ref[...], wd_ref[...])
        h_ref[...] = hidden(wg_ref[...], wu_ref[...])

    @pl.when(j == nj)
    def _():
        acc = o_ref[...] + _dot(h_ref[...], wd_ref[...])
        if has_tail:
            acc = acc + _dot(hidden(wgt_ref[...], wut_ref[...]), wdt_ref[...])
        o_ref[...] = x_ref[...] + 0.5 * _rms(acc, gpost_ref[...])


def _ffn(h, gains, wg, wu, wd, layer, idx, k_pre, k_post, tm, tf):
    m, d = h.shape
    dff = wg.shape[-1]
    nj = dff // tf
    tail = dff - nj * tf
    assert m % tm == 0 and tail in (0, LANE) and nj >= 1
    tail_blk = dff // LANE - 1
    one = pl.Buffered(1)
    in_specs = [
        pl.BlockSpec((tm, d), lambda i, j: (i, 0), pipeline_mode=one),
        pl.BlockSpec((None, None, 1, d), lambda i, j: (layer, k_pre, 0, 0)),
        pl.BlockSpec((None, None, 1, d), lambda i, j: (layer, k_post, 0, 0)),
        pl.BlockSpec((None, None, d, tf), lambda i, j: (layer, idx, 0, jnp.minimum(j, nj - 1))),
        pl.BlockSpec((None, None, d, tf), lambda i, j: (layer, idx, 0, jnp.minimum(j, nj - 1))),
        pl.BlockSpec((None, None, tf, d), lambda i, j: (layer, idx, jnp.maximum(j - 1, 0), 0)),
    ]
    args = [h, gains, gains, wg, wu, wd]
    if tail:
        in_specs += [
            pl.BlockSpec((None, None, d, LANE), lambda i, j: (layer, idx, 0, tail_blk), pipeline_mode=one),
            pl.BlockSpec((None, None, d, LANE), lambda i, j: (layer, idx, 0, tail_blk), pipeline_mode=one),
            pl.BlockSpec((None, None, LANE, d), lambda i, j: (layer, idx, tail_blk, 0), pipeline_mode=one),
        ]
        args += [wg, wu, wd]
    return pl.pallas_call(
        functools.partial(_ffn_body, bool(tail)),
        grid=(m // tm, nj + 1),
        in_specs=in_specs,
        out_specs=pl.BlockSpec((tm, d), lambda i, j: (i, 0)),
        out_shape=jax.ShapeDtypeStruct((m, d), F32),
        scratch_shapes=[pltpu.VMEM((tm, d), BF16), pltpu.VMEM((tm, tf), BF16)],
        compiler_params=_params(2, 58),
        name="ffn",
    )(*args)


def _nmm_body(x_ref, g_ref, w_ref, o_ref, xn_ref):
    @pl.when(pl.program_id(1) == 0)
    def _():
        xn_ref[...] = _rms(x_ref[...], g_ref[...]).astype(BF16)

    o_ref[...] = _dot(xn_ref[...], w_ref[...])


def _norm_matmul(h, gains, layer, k_gain, w, j_w, tm, tn):
    m, d = h.shape
    n = w.shape[-1]
    assert m % tm == 0 and n % tn == 0
    return pl.pallas_call(
        _nmm_body,
        grid=(m // tm, n // tn),
        in_specs=[
            pl.BlockSpec((tm, d), lambda i, j: (i, 0), pipeline_mode=pl.Buffered(1)),
            pl.BlockSpec((None, None, 1, d), lambda i, j: (layer, k_gain, 0, 0)),
            pl.BlockSpec((None, d, tn), lambda i, j: (j_w, 0, j)),
        ],
        out_specs=pl.BlockSpec((tm, tn), lambda i, j: (i, j)),
        out_shape=jax.ShapeDtypeStruct((m, n), F32),
        scratch_shapes=[pltpu.VMEM((tm, d), BF16)],
        compiler_params=_params(2, 48),
        name="norm_matmul",
    )(h, gains, w)


def _mmpn_body(steps, *refs):
    n_in = len(steps)
    y_refs = refs[:n_in]
    w_ref, h_ref, g_ref, o_ref = refs[n_in:]
    k = pl.program_id(1)
    last = pl.num_programs(1) - 1

    @pl.when(k == 0)
    def _():
        o_ref[...] = jnp.zeros_like(o_ref)

    s0 = 0
    for y_ref, ns in zip(y_refs, steps):
        @pl.when((k >= s0) & (k < s0 + ns))
        def _(y_ref=y_ref):
            o_ref[...] += _dot(y_ref[...], w_ref[...])
        s0 += ns

    @pl.when(k == last)
    def _():
        o_ref[...] = h_ref[...] + _rms(o_ref[...], g_ref[...])


def _matmul_postnorm(ys, w, j_w, h, gains, layer, k_gain, tm, tk):
    m, d = h.shape
    steps = tuple(y.shape[1] // tk for y in ys)
    assert all(y.shape[1] % tk == 0 for y in ys) and sum(steps) * tk == w.shape[1] and m % tm == 0
    in_specs = []
    s0 = 0
    for ns in steps:
        in_specs.append(pl.BlockSpec(
            (tm, tk), lambda i, k, s0=s0, ns=ns: (i, jnp.clip(k - s0, 0, ns - 1))))
        s0 += ns
    in_specs += [
        pl.BlockSpec((None, tk, d), lambda i, k: (j_w, k, 0)),
        pl.BlockSpec((tm, d), lambda i, k: (i, 0), pipeline_mode=pl.Buffered(1)),
        pl.BlockSpec((None, None, 1, d), lambda i, k: (layer, k_gain, 0, 0)),
    ]
    return pl.pallas_call(
        functools.partial(_mmpn_body, steps),
        grid=(m // tm, sum(steps)),
        in_specs=in_specs,
        out_specs=pl.BlockSpec((tm, d), lambda i, k: (i, 0)),
        out_shape=jax.ShapeDtypeStruct((m, d), F32),
        compiler_params=_params(2, 56),
        name="matmul_postnorm",
    )(*ys, w, h, gains)


def _s5_param_body(tt, lre_ref, lim_ref, ls_ref, br_ref, bi_ref,
                   ap2r_ref, ap2i_ref, tabr_ref, tabi_ref, bbr_ref, bbi_ref):
    lr = jnp.minimum(lre_ref[...], -1e-4)
    li = lim_ref[...]
    step = jnp.exp(ls_ref[...])
    mag = jnp.exp(lr * step)
    ang = li * step
    ar = mag * jnp.cos(ang)
    ai = mag * jnp.sin(ang)
    den = lr * lr + li * li
    nr = ar - 1.0
    gr = (nr * lr + ai * li) / den
    gi = (ai * lr - nr * li) / den
    br = br_ref[...]
    bi = bi_ref[...]
    bbr_ref[...] = gr * br - gi * bi
    bbi_ref[...] = gr * bi + gi * br
    pr, pi = ar, ai
    tabr_ref[0:1, :] = ar
    tabi_ref[0:1, :] = ai
    for k in range(ap2r_ref.shape[0]):
        ap2r_ref[k:k + 1, :] = pr
        ap2i_ref[k:k + 1, :] = pi
        n = 2 ** k
        if n < tt:
            tr = tabr_ref[0:n, :]
            ti = tabi_ref[0:n, :]
            tabr_ref[n:2 * n, :] = tr * pr - ti * pi
            tabi_ref[n:2 * n, :] = tr * pi + ti * pr
        pr, pi = pr * pr - pi * pi, pr * pi + pi * pr


def _s5_params(lam_re, lam_im, log_step, b_re, b_im, tt):
    n = lam_re.shape[-1]
    c = b_re.shape[0]
    full = lambda shape: pl.BlockSpec(shape, lambda: (0,) * len(shape))
    return pl.pallas_call(
        functools.partial(_s5_param_body, tt),
        in_specs=[full((1, n))] * 3 + [full((c, n))] * 2,
        out_specs=[full((8, n))] * 2 + [full((tt, n))] * 2 + [full((c, n))] * 2,
        out_shape=[jax.ShapeDtypeStruct((8, n), F32)] * 2 + [jax.ShapeDtypeStruct((tt, n), F32)] * 2
        + [jax.ShapeDtypeStruct((c, n), F32)] * 2,
        compiler_params=pltpu.CompilerParams(vmem_limit_bytes=32 * MIB),
        name="s5_params",
    )(lam_re, lam_im, log_step, b_re, b_im)


def _s5_body(scan, n_chunks, *refs):
    if scan:
        (u_ref, ap2r_ref, ap2i_ref, tabr_ref, tabi_ref, bbr_ref, bbi_ref, ccr_ref, cci_ref, d_ref,
         gw_ref, gb_ref, y_ref, sr_ref, si_ref, cr_s, ci_s) = refs
    else:
        (u_ref, ap2r_ref, ap2i_ref, bbr_ref, bbi_ref, ccr_ref, cci_ref, d_ref,
         gw_ref, gb_ref, h0r_ref, h0i_ref, y_ref, sr_ref, si_ref) = refs
    tt = u_ref.shape[0]
    cw = u_ref.shape[1] // n_chunks
    sw = bbr_ref.shape[2]

    if scan:
        @pl.when(pl.program_id(1) == 0)
        def _():
            cr_s[...] = jnp.zeros_like(cr_s)
            ci_s[...] = jnp.zeros_like(ci_s)

    u = u_ref[...]
    ys = []
    for c in range(n_chunks):
        uc = u[:, c * cw:(c + 1) * cw]
        lanes = slice(c * sw, (c + 1) * sw)
        xr = _dot(uc, bbr_ref[c])
        xi = _dot(uc, bbi_ref[c])
        if scan:
            for k in range(3):
                pr = ap2r_ref[k:k + 1, lanes]
                pi = ap2i_ref[k:k + 1, lanes]
                sr = _shift_rows_in_groups(xr, 2 ** k)
                si = _shift_rows_in_groups(xi, 2 ** k)
                xr, xi = xr + (pr * sr - pi * si), xi + (pr * si + pi * sr)
            tr = tabr_ref[:, lanes]
            ti = tabi_ref[:, lanes]
            hr = cr_s[:, lanes]
            hi = ci_s[:, lanes]
            out_r, out_i = [], []
            for g in range(tt // 8):
                gr = xr[8 * g:8 * g + 8] + (tr * hr - ti * hi)
                gi = xi[8 * g:8 * g + 8] + (tr * hi + ti * hr)
                out_r.append(gr)
                out_i.append(gi)
                hr, hi = gr[7:8], gi[7:8]
            xr = jnp.concatenate(out_r, axis=0)
            xi = jnp.concatenate(out_i, axis=0)
            cr_s[:, lanes] = hr
            ci_s[:, lanes] = hi
        else:
            tr = ap2r_ref[0:1, lanes]
            ti = ap2i_ref[0:1, lanes]
            hr = h0r_ref[:, lanes]
            hi = h0i_ref[:, lanes]
            xr, xi = xr + (tr * hr - ti * hi), xi + (tr * hi + ti * hr)
            sr_ref[:, lanes] = xr
            si_ref[:, lanes] = xi
        ys.append(_dot(xr, ccr_ref[c]) - _dot(xi, cci_ref[c]) + d_ref[:, c * cw:(c + 1) * cw] * uc)
    y = _gelu_tanh(jnp.concatenate(ys, axis=-1))
    y_ref[...] = y * _sigmoid(_dot(y, gw_ref[...]) + gb_ref[...])
    if scan:
        sr_ref[0] = cr_s[...]
        si_ref[0] = ci_s[...]


def _s5_specs(j, width, n_chunks, cw, sw, n_state, nidx):
    zero = (0,) * nidx
    return dict(
        ap2=pl.BlockSpec((8, n_state), lambda *a: (0, 0)),
        bb=pl.BlockSpec((n_chunks, cw, sw), lambda *a: (0, 0, 0)),
        cc=pl.BlockSpec((n_chunks, sw, cw), lambda *a: (0, 0, 0)),
        d=pl.BlockSpec((None, 1, width), lambda *a: (j, 0, 0)),
        gw=pl.BlockSpec((None, width, width), lambda *a: (j, 0, 0)),
        gb=pl.BlockSpec((None, 1, width), lambda *a: (j, 0, 0)),
    )


def _s5_seq(proj, row0, bsz, t, tt, j, prm, d3, glu_w, glu_b3):
    ap2r, ap2i, tabr, tabi, bbr, bbi, ccr, cci = prm
    n_chunks, cw, sw = bbr.shape
    width = n_chunks * cw
    n_state = n_chunks * sw
    nt = t // tt
    rb0 = row0 // tt
    sp = _s5_specs(j, width, n_chunks, cw, sw, n_state, 2)
    tab = pl.BlockSpec((8, n_state), lambda b, i: (0, 0))
    st = pl.BlockSpec((1, 1, n_state), lambda b, i: (b, 0, 0))
    y, sr, si = pl.pallas_call(
        functools.partial(_s5_body, True, n_chunks),
        grid=(bsz, nt),
        in_specs=[pl.BlockSpec((tt, width), lambda b, i: (rb0 + b * nt + i, 0)),
                  sp["ap2"], sp["ap2"], tab, tab, sp["bb"], sp["bb"], sp["cc"], sp["cc"],
                  sp["d"], sp["gw"], sp["gb"]],
        out_specs=[pl.BlockSpec((tt, width), lambda b, i: (b * nt + i, 0)), st, st],
        out_shape=[jax.ShapeDtypeStruct((bsz * t, width), F32),
                   jax.ShapeDtypeStruct((bsz, 1, n_state), F32),
                   jax.ShapeDtypeStruct((bsz, 1, n_state), F32)],
        scratch_shapes=[pltpu.VMEM((1, n_state), F32)] * 2,
        compiler_params=_params(2, 48),
        name="s5_seq",
    )(proj, ap2r, ap2i, tabr, tabi, bbr, bbi, ccr, cci, d3, glu_w, glu_b3)
    return y, sr[:, 0], si[:, 0]


def _s5_step(proj, row0, bsz, j, prm, d3, glu_w, glu_b3, h0r, h0i):
    ap2r, ap2i, _, _, bbr, bbi, ccr, cci = prm
    n_chunks, cw, sw = bbr.shape
    width = n_chunks * cw
    n_state = n_chunks * sw
    rb0 = row0 // bsz
    sp = _s5_specs(j, width, n_chunks, cw, sw, n_state, 1)
    st = pl.BlockSpec((bsz, n_state), lambda i: (0, 0))
    return pl.pallas_call(
        functools.partial(_s5_body, False, n_chunks),
        grid=(1,),
        in_specs=[pl.BlockSpec((bsz, width), lambda i: (rb0, 0)),
                  sp["ap2"], sp["ap2"], sp["bb"], sp["bb"], sp["cc"], sp["cc"],
                  sp["d"], sp["gw"], sp["gb"], st, st],
        out_specs=[pl.BlockSpec((bsz, width), lambda i: (0, 0)), st, st],
        out_shape=[jax.ShapeDtypeStruct((bsz, width), F32),
                   jax.ShapeDtypeStruct((bsz, n_state), F32),
                   jax.ShapeDtypeStruct((bsz, n_state), F32)],
        compiler_params=_params(1, 48),
        name="s5_step",
    )(proj, ap2r, ap2i, bbr, bbi, ccr, cci, d3, glu_w, glu_b3, h0r, h0i)


def _hgrn_lower_bound(logits, j, axis):
    n = logits.shape[axis]
    mx = jnp.max(logits, axis=axis, keepdims=True)
    e = jnp.exp(logits - mx)
    sm = e / jnp.sum(e, axis=axis, keepdims=True)
    pick = (lambda i: sm[i:i + 1, :]) if axis == 0 else (lambda i: sm[:, i:i + 1])
    cum = pick(0)
    for i in range(1, j + 1):
        cum = cum + pick(i)
    assert j < n
    return cum - pick(0)


def _hgrn_gates(fl, lb):
    logf = _logaddexp(_log_sigmoid(fl), jnp.log(lb) + _log_sigmoid(-fl))
    k = (1.0 - lb) * _sigmoid(-fl)
    return logf, k


def _hgrn_seq_body(j, chunk, sub, q_ref, f_ref, v_ref, gt_ref, lg_ref, gain_ref,
                   y_ref, s_ref, st_s, q_s, k_s, v_s, b_s, o_s):
    tt, dk = q_ref.shape
    i = pl.program_id(2)

    @pl.when(i == 0)
    def _():
        st_s[...] = jnp.zeros_like(st_s)

    lb = _hgrn_lower_bound(lg_ref[...], j, 0)
    logf, k = _hgrn_gates(f_ref[...], lb)
    rin = lax.broadcasted_iota(jnp.int32, (tt, dk), 0) & (chunk - 1)
    b = logf
    s = 1
    while s < chunk:
        b = b + jnp.where(rin >= s, _shift_rows(b, s), 0.0)
        s *= 2
    q_s[...] = _silu(q_ref[...])
    k_s[...] = k
    v_s[...] = v_ref[...]
    b_s[...] = b
    nsub = chunk // sub
    tsub = lax.broadcasted_iota(jnp.int32, (sub, dk), 0)

    def one_chunk(c, carry):
        r0 = pl.multiple_of(c * chunk, chunk)
        qc = q_s[pl.ds(r0, chunk), :]
        kc = k_s[pl.ds(r0, chunk), :]
        vc = v_s[pl.ds(r0, chunk), :]
        bc = b_s[pl.ds(r0, chunk), :]
        acc = [jnp.zeros((sub, dk), F32) for _ in range(nsub)]
        for s_i in range(chunk):
            ks = kc[s_i:s_i + 1]
            bs = bc[s_i:s_i + 1]
            vs = vc[s_i:s_i + 1]
            sb = s_i // sub
            for tb in range(sb, nsub):
                diff = bc[tb * sub:(tb + 1) * sub] - bs
                if tb == sb:
                    diff = jnp.where(tsub >= s_i - sb * sub, diff, -jnp.inf)
                p = qc[tb * sub:(tb + 1) * sub] * ks * jnp.exp(diff)
                acc[tb] = acc[tb] + jnp.sum(p, axis=-1, keepdims=True) * vs
        st = st_s[...]
        o = jnp.concatenate(acc, axis=0) + _dot_nt(qc * jnp.exp(bc), st)
        b_last = bc[chunk - 1:chunk]
        st_s[...] = st * jnp.exp(b_last) + _dot_tn(vc, kc * jnp.exp(b_last - bc))
        o_s[pl.ds(r0, chunk), :] = o
        return carry

    lax.fori_loop(0, tt // chunk, one_chunk, 0, unroll=2)
    o = o_s[...]
    o = o * lax.rsqrt(jnp.mean(o * o, axis=-1, keepdims=True) + EPS)
    y_ref[...] = o * gain_ref[...] * _silu(gt_ref[...])

    @pl.when(i == pl.num_programs(2) - 1)
    def _():
        s_ref[...] = st_s[...].T


def _hgrn_seq(proj, row0, bsz, t, tt, j, col0, logits, gain3, chunk=32, sub=8):
    h, dk = HG_HEADS, LANE
    nt = t // tt
    rb0 = row0 // tt
    cb = col0 // dk
    sec = lambda n: pl.BlockSpec((tt, dk), lambda b, hh, i: (rb0 + b * nt + i, cb + n * h + hh))
    y, s = pl.pallas_call(
        functools.partial(_hgrn_seq_body, j, chunk, sub),
        grid=(bsz, h, nt),
        in_specs=[sec(0), sec(1), sec(2), sec(3),
                  pl.BlockSpec((logits.shape[0], dk), lambda b, hh, i: (0, hh)),
                  pl.BlockSpec((None, 1, dk), lambda b, hh, i: (j, 0, hh))],
        out_specs=[pl.BlockSpec((tt, dk), lambda b, hh, i: (b * nt + i, hh)),
                   pl.BlockSpec((None, None, dk, dk), lambda b, hh, i: (b, hh, 0, 0))],
        out_shape=[jax.ShapeDtypeStruct((bsz * t, h * dk), F32),
                   jax.ShapeDtypeStruct((bsz, h, dk, dk), F32)],
        scratch_shapes=[pltpu.VMEM((dk, dk), F32)] + [pltpu.VMEM((tt, dk), F32)] * 5,
        compiler_params=_params(3, 32),
        name="hgrn_seq",
    )(proj, proj, proj, proj, logits, gain3)
    return y, s


def _hgrn_step_body(j, bb, q_ref, f_ref, v_ref, gt_ref, qc_ref, fc_ref, lg_ref, lgt_ref, gain_ref,
                    s_ref, *rest):
    y_ref, so_ref = rest[-2:]
    lb_r = _hgrn_lower_bound(lg_ref[...], j, 0)
    lb_c = _hgrn_lower_bound(lgt_ref[...], j, 1)
    qcol = _silu(qc_ref[...])
    logf_c, kcol = _hgrn_gates(fc_ref[...], lb_c)
    fcol = jnp.exp(logf_c)
    for n in range(bb):
        s = s_ref[n]
        q_r = _silu(q_ref[n])
        _, k_r = _hgrn_gates(f_ref[n], lb_r)
        v_r = v_ref[n]
        q_c = qcol[:, n:n + 1]
        f_c = fcol[:, n:n + 1]
        k_c = kcol[:, n:n + 1]
        score = jnp.sum(q_r * k_r, axis=-1, keepdims=True)
        o = score * v_r + jnp.sum((q_c * f_c) * s, axis=0, keepdims=True)
        so_ref[n] = f_c * s + k_c * v_r
        o = o * lax.rsqrt(jnp.mean(o * o, axis=-1, keepdims=True) + EPS)
        y_ref[n] = o * gain_ref[...] * _silu(gt_ref[n])


def _stacked_out(prev, in_specs, args, out_index):
    if prev is None:
        return {}
    in_specs.append(pl.BlockSpec(memory_space=pl.ANY))
    args.append(prev)
    return {len(args) - 1: out_index}


def _hgrn_step(proj3, proj_t, j, col0, logits, logits_t, gain3, s_all, s_prev, bb=8):
    bsz = proj3.shape[0]
    h, dk = HG_HEADS, LANE
    cb = col0 // dk
    qt, ft = proj_t
    row = lambda n: pl.BlockSpec((bb, 1, dk), lambda hh, i: (i, 0, cb + n * h + hh))
    col = pl.BlockSpec((None, None, dk, bb), lambda hh, i: (hh, i, 0, 0))
    st = pl.BlockSpec((None, bb, None, dk, dk), lambda hh, i: (j, i, hh, 0, 0))
    in_specs = [row(0), row(1), row(2), row(3), col, col,
                pl.BlockSpec((logits.shape[0], dk), lambda hh, i: (0, hh)),
                pl.BlockSpec((dk, logits.shape[0]), lambda hh, i: (hh, 0)),
                pl.BlockSpec((None, 1, dk), lambda hh, i: (j, 0, hh)),
                st]
    args = [proj3, proj3, proj3, proj3, qt, ft, logits, logits_t, gain3, s_all]
    aliases = _stacked_out(s_prev, in_specs, args, 1)
    y, s = pl.pallas_call(
        functools.partial(_hgrn_step_body, j, bb),
        grid=(h, bsz // bb),
        in_specs=in_specs,
        out_specs=[pl.BlockSpec((bb, 1, dk), lambda hh, i: (i, 0, hh)), st],
        out_shape=[jax.ShapeDtypeStruct((bsz, 1, h * dk), F32),
                   jax.ShapeDtypeStruct(s_all.shape, F32)],
        input_output_aliases=aliases,
        compiler_params=_params(2, 32),
        name="hgrn_step",
    )(*args)
    return y.reshape(bsz, h * dk), s


def _ml_qkv_gates(c, xc, xm, bdq_ref, bdk_ref, bdv_ref, wg_ref, bg_ref, q_ref, k_ref, v_ref, xc_ref, g_ref):
    nblk = bdq_ref.shape[0]
    w = bdq_ref.shape[1]
    xcb = xc.astype(BF16)
    xmb = xm.astype(BF16)
    q = jnp.concatenate([_dot(xcb[:, s * w:(s + 1) * w], bdq_ref[s]) for s in range(nblk)], axis=-1)
    k = jnp.concatenate([_dot(xcb[:, s * w:(s + 1) * w], bdk_ref[s]) for s in range(nblk)], axis=-1)
    v = jnp.concatenate([_dot(xmb[:, s * w:(s + 1) * w], bdv_ref[s]) for s in range(nblk)], axis=-1)
    q_ref[...] = q
    k_ref[...] = k
    v_ref[...] = v
    xc_ref[...] = xc
    g = _dot(q, wg_ref[0]) + _dot(k, wg_ref[1]) + _dot(v, wg_ref[2])

    @pl.when(c == 0)
    def _():
        g_ref[...] = g + bg_ref[...]

    @pl.when(c > 0)
    def _():
        g_ref[...] += g


def _ml_pre_seq_body(xm_ref, cw_ref, cb_ref, bdq_ref, bdk_ref, bdv_ref, wg_ref, bg_ref,
                     q_ref, k_ref, v_ref, xc_ref, g_ref, stage_s, halo_s):
    i = pl.program_id(1)
    c = pl.program_id(2)
    tt = xm_ref.shape[0]

    @pl.when(i == 0)
    def _():
        halo_s[c] = jnp.zeros(halo_s.shape[1:], F32)

    xm = xm_ref[...]
    stage_s[0:8, :] = halo_s[c]
    stage_s[8:, :] = xm
    halo_s[c] = xm[tt - 8:tt]
    acc = cb_ref[...] + cw_ref[ML_CONV - 1:ML_CONV, :] * xm
    for d in range(1, ML_CONV):
        acc = acc + cw_ref[ML_CONV - 1 - d:ML_CONV - d, :] * stage_s[8 - d:8 - d + tt, :]
    _ml_qkv_gates(c, _silu(acc), xm, bdq_ref, bdk_ref, bdv_ref, wg_ref, bg_ref,
                  q_ref, k_ref, v_ref, xc_ref, g_ref)


def _ml_pre_step_body(xm_ref, taps_ref, cw_ref, cb_ref, bdq_ref, bdk_ref, bdv_ref, wg_ref, bg_ref,
                      q_ref, k_ref, v_ref, xc_ref, g_ref):
    c = pl.program_id(0)
    xm = xm_ref[...]
    acc = cb_ref[...] + cw_ref[ML_CONV - 1:ML_CONV, :] * xm
    for d in range(1, ML_CONV):
        acc = acc + cw_ref[ML_CONV - 1 - d:ML_CONV - d, :] * taps_ref[ML_CONV - 1 - d]
    _ml_qkv_gates(c, _silu(acc), xm, bdq_ref, bdk_ref, bdv_ref, wg_ref, bg_ref,
                  q_ref, k_ref, v_ref, xc_ref, g_ref)


def _ml_pre(up, row0, bsz, t, tt, lc, j, conv_w, conv_b3, bd, wg4, bg3, taps=None):
    inner = conv_w.shape[-1]
    ng = wg4.shape[-1]
    nc = inner // lc
    nblk = lc // MXU_DIM
    bdq, bdk, bdv = bd
    seq = taps is None
    rows = bsz * t
    if seq:
        nt = t // tt
        rb0 = row0 // tt
        grid = (bsz, nt, nc)
        rmap = lambda b, i, c: b * nt + i
        cmap = lambda b, i, c: c
    else:
        tt = bsz
        rb0 = row0 // tt
        grid = (nc,)
        rmap = lambda c: 0
        cmap = lambda c: c
    in_specs = [pl.BlockSpec((tt, lc), lambda *a: (rb0 + rmap(*a), cmap(*a)))]
    args = [up]
    if not seq:
        in_specs.append(pl.BlockSpec((ML_CONV - 1, tt, lc), lambda *a: (0, 0, cmap(*a))))
        args.append(taps)
    in_specs += [
        pl.BlockSpec((None, ML_CONV, lc), lambda *a: (j, 0, cmap(*a))),
        pl.BlockSpec((None, 1, lc), lambda *a: (j, 0, cmap(*a))),
        pl.BlockSpec((None, nblk, MXU_DIM, MXU_DIM), lambda *a: (j, cmap(*a), 0, 0)),
        pl.BlockSpec((None, nblk, MXU_DIM, MXU_DIM), lambda *a: (j, cmap(*a), 0, 0)),
        pl.BlockSpec((None, nblk, MXU_DIM, MXU_DIM), lambda *a: (j, cmap(*a), 0, 0)),
        pl.BlockSpec((None, 3, lc, ng), lambda *a: (j, 0, cmap(*a), 0)),
        pl.BlockSpec((None, 1, ng), lambda *a: (j, 0, 0)),
    ]
    args += [conv_w, conv_b3, bdq, bdk, bdv, wg4, bg3]
    big = pl.BlockSpec((tt, lc), lambda *a: (rmap(*a), cmap(*a)))
    return pl.pallas_call(
        _ml_pre_seq_body if seq else _ml_pre_step_body,
        grid=grid,
        in_specs=in_specs,
        out_specs=[big, big, big, big, pl.BlockSpec((tt, ng), lambda *a: (rmap(*a), 0))],
        out_shape=[jax.ShapeDtypeStruct((rows, inner), F32)] * 4 + [jax.ShapeDtypeStruct((rows, ng), F32)],
        scratch_shapes=([pltpu.VMEM((tt + 8, lc), F32), pltpu.VMEM((nc, 8, lc), F32)] if seq else []),
        compiler_params=_params(len(grid), 48),
        name="ml_pre_seq" if seq else "ml_pre_step",
    )(*args)


def _ml_out(h, gain, skip, xc, z):
    mu = jnp.mean(h, axis=-1, keepdims=True)
    hc = h - mu
    hn = hc * lax.rsqrt(jnp.mean(hc * hc, axis=-1, keepdims=True) + EPS)
    return (hn * gain + skip * xc) * _silu(z)


def _ml_seq_body(q_ref, k_ref, v_ref, xc_ref, z_ref, g_ref, gt_ref, gain_ref, skip_ref,
                 o_ref, c_ref, n_ref, m_ref):
    hh = pl.program_id(1)
    i = pl.program_id(2)
    L, dh = q_ref.shape
    nh = g_ref.shape[1] // 2

    @pl.when(i == 0)
    def _():
        c_ref[...] = jnp.zeros_like(c_ref)
        n_ref[...] = jnp.zeros_like(n_ref)
        m_ref[...] = jnp.zeros_like(m_ref)

    g = g_ref[...]
    gt = gt_ref[...]
    lane = lax.broadcasted_iota(jnp.int32, g.shape, 1)
    sub = lax.broadcasted_iota(jnp.int32, gt.shape, 0)
    ig_c = jnp.sum(jnp.where(lane == hh, g, 0.0), axis=1, keepdims=True)
    lf_c = _log_sigmoid(jnp.sum(jnp.where(lane == nh + hh, g, 0.0), axis=1, keepdims=True))
    ig_r = jnp.sum(jnp.where(sub == hh, gt, 0.0), axis=0, keepdims=True)
    lf_r = _log_sigmoid(jnp.sum(jnp.where(sub == nh + hh, gt, 0.0), axis=0, keepdims=True))
    tr = lax.broadcasted_iota(jnp.int32, (L, L), 0)
    tc = lax.broadcasted_iota(jnp.int32, (L, L), 1)
    causal = tr >= tc
    b_c = jnp.sum(jnp.where(causal, lf_r, 0.0), axis=1, keepdims=True)
    b_r = jnp.sum(jnp.where(tr <= tc, lf_c, 0.0), axis=0, keepdims=True)
    m = m_ref[...]
    dlog = jnp.where(causal, b_c - b_r + ig_r, -jnp.inf)
    gsum = b_c + m
    mt = jnp.maximum(gsum, jnp.max(dlog, axis=1, keepdims=True))
    w = jnp.exp(dlog - mt)
    gi = jnp.exp(gsum - mt)
    q = q_ref[...]
    k = k_ref[...] * (dh ** -0.5)
    v = v_ref[...]
    c_old = c_ref[...]
    n_old = n_ref[...]
    qk = _dot_nt(q, k) * w
    num = _dot(qk, v) + gi * _dot(q, c_old)
    den = jnp.sum(qk, axis=1, keepdims=True) + gi * jnp.sum(q * n_old, axis=1, keepdims=True)
    h = num / jnp.maximum(jnp.abs(den), jnp.exp(-mt))
    m_new = mt[L - 1:L]
    b_last = b_c[L - 1:L]
    decay = jnp.exp(b_last + m - m_new)
    ks = k * jnp.exp(b_last - b_c + ig_c - m_new)
    c_ref[...] = decay * c_old + _dot_tn(ks, v)
    n_ref[...] = decay * n_old + jnp.sum(ks, axis=0, keepdims=True)
    m_ref[...] = m_new
    o_ref[...] = _ml_out(h, gain_ref[...], skip_ref[...], xc_ref[...], z_ref[...])


def _ml_seq(q, k, v, xc, up, row0, gates, gates_t, bsz, t, L, j, gain3, skip3):
    inner = q.shape[1]
    nh = ML_HEADS
    dh = inner // nh
    nt = t // L
    rb0 = row0 // L
    blk = pl.BlockSpec((L, dh), lambda b, hh, i: (b * nt + i, hh))
    par = pl.BlockSpec((None, 1, dh), lambda b, hh, i: (j, 0, hh))
    return pl.pallas_call(
        _ml_seq_body,
        grid=(bsz, nh, nt),
        in_specs=[blk, blk, blk, blk,
                  pl.BlockSpec((L, dh), lambda b, hh, i: (rb0 + b * nt + i, nh + hh)),
                  pl.BlockSpec((L, 2 * nh), lambda b, hh, i: (b * nt + i, 0)),
                  pl.BlockSpec((2 * nh, L), lambda b, hh, i: (0, b * nt + i)),
                  par, par],
        out_specs=[blk,
                   pl.BlockSpec((None, None, dh, dh), lambda b, hh, i: (b, hh, 0, 0)),
                   pl.BlockSpec((None, None, 1, dh), lambda b, hh, i: (b, hh, 0, 0)),
                   pl.BlockSpec((None, None, 1, 1), lambda b, hh, i: (b, hh, 0, 0))],
        out_shape=[jax.ShapeDtypeStruct((bsz * t, inner), F32),
                   jax.ShapeDtypeStruct((bsz, nh, dh, dh), F32),
                   jax.ShapeDtypeStruct((bsz, nh, 1, dh), F32),
                   jax.ShapeDtypeStruct((bsz, nh, 1, 1), F32)],
        compiler_params=_params(3, 48),
        name="ml_seq",
    )(q, k, v, xc, up, gates, gates_t, gain3, skip3)


def _ml_step_body(bb, q_ref, k_ref, v_ref, xc_ref, z_ref, qc_ref, kc_ref, ig_ref, lf_ref, m_ref,
                  gain_ref, skip_ref, c_ref, n_ref, *rest):
    o_ref, co_ref, no_ref, mo_ref = rest[-4:]
    dh = q_ref.shape[-1]
    scale = dh ** -0.5
    ig_all = ig_ref[...]
    lf_all = _log_sigmoid(lf_ref[...])
    m_all = m_ref[...]
    g_all = lf_all + m_all
    mt_all = jnp.maximum(g_all, ig_all)
    w_all = jnp.exp(ig_all - mt_all)
    gi_all = jnp.exp(g_all - mt_all)
    mo_ref[...] = mt_all
    for n in range(bb):
        c_old = c_ref[n]
        n_old = n_ref[n]
        q_r = q_ref[n]
        k_r = k_ref[n] * scale
        v_r = v_ref[n]
        q_c = qc_ref[:, n:n + 1]
        k_c = kc_ref[:, n:n + 1] * scale
        mt = mt_all[:, n:n + 1]
        w = w_all[:, n:n + 1]
        gi = gi_all[:, n:n + 1]
        qk = jnp.sum(q_r * k_r, axis=-1, keepdims=True) * w
        num = qk * v_r + gi * jnp.sum(q_c * c_old, axis=0, keepdims=True)
        den = qk + gi * jnp.sum(q_r * n_old, axis=-1, keepdims=True)
        h = num / jnp.maximum(jnp.abs(den), jnp.exp(-mt))
        co_ref[n] = gi * c_old + (w * k_c) * v_r
        no_ref[n] = gi * n_old + w * k_r
        o_ref[n] = _ml_out(h, gain_ref[...], skip_ref[...], xc_ref[n], z_ref[n])


def _ml_step(q3, k3, v3, xc3, up3, q_t, k_t, ig_t, lf_t, m_t, j, gain3, skip3, c_all, n_all, c_prev, bb=4):
    bsz, _, inner = q3.shape
    nh = ML_HEADS
    dh = inner // nh
    row = pl.BlockSpec((bb, 1, dh), lambda hh, i: (i, 0, hh))
    col = pl.BlockSpec((None, None, dh, bb), lambda hh, i: (hh, i, 0, 0))
    sc = pl.BlockSpec((None, None, 1, bb), lambda hh, i: (hh, i, 0, 0))
    par = pl.BlockSpec((None, 1, dh), lambda hh, i: (j, 0, hh))
    nst = pl.BlockSpec((bb, None, 1, dh), lambda hh, i: (i, hh, 0, 0))
    cst = pl.BlockSpec((None, bb, None, dh, dh), lambda hh, i: (j, i, hh, 0, 0))
    nst_in = pl.BlockSpec((None, bb, None, 1, dh), lambda hh, i: (j, i, hh, 0, 0))
    in_specs = [row, row, row, row,
                pl.BlockSpec((bb, 1, dh), lambda hh, i: (i, 0, nh + hh)),
                col, col, sc, sc, sc, par, par, cst, nst_in]
    args = [q3, k3, v3, xc3, up3, q_t, k_t, ig_t, lf_t, m_t, gain3, skip3, c_all, n_all]
    aliases = _stacked_out(c_prev, in_specs, args, 1)
    return pl.pallas_call(
        functools.partial(_ml_step_body, bb),
        grid=(nh, bsz // bb),
        in_specs=in_specs,
        out_specs=[row, cst, nst, sc],
        out_shape=[jax.ShapeDtypeStruct((bsz, 1, inner), F32),
                   jax.ShapeDtypeStruct(c_all.shape, F32),
                   jax.ShapeDtypeStruct(n_all.shape[1:], F32),
                   jax.ShapeDtypeStruct(m_t.shape, F32)],
        input_output_aliases=aliases,
        compiler_params=_params(2, 48),
        name="ml_step",
    )(*args)


def _block_diag(w, size):
    n, c, _ = w.shape
    per = size // c
    w4 = w.reshape(n // per, per, c, c)
    eye = jnp.eye(per, dtype=w.dtype)
    out = w4[:, :, :, None, :] * eye[None, :, None, :, None]
    return out.reshape(n // per, size, size).astype(BF16)


def _to_cols(x, nh, bb):
    bsz, w = x.shape
    d = w // nh
    return x.reshape(bsz // bb, bb, nh, d).transpose(2, 0, 3, 1)


def _s5_layout(bbt_r, bbt_i, c_re, c_im, n_chunks):
    ch, n_state = bbt_r.shape
    g = c_re.shape[0]
    p = n_state // g
    per = g // n_chunks
    eye = jnp.eye(per, dtype=F32)

    def bmat(bt):
        b4 = bt.reshape(ch, n_chunks, per, p).transpose(1, 0, 2, 3)
        out = eye[None, :, None, :, None] * b4[:, None, :, :, :]
        return out.reshape(n_chunks, per * ch, per * p).astype(BF16)

    def cmat(c):
        c4 = c.reshape(n_chunks, per, ch, p).transpose(0, 1, 3, 2)
        out = c4[:, :, :, None, :] * eye[None, :, None, :, None]
        return out.reshape(n_chunks, per * p, per * ch).astype(BF16)

    return bmat(bbt_r), bmat(bbt_i), cmat(c_re), cmat(c_im)


TM_FFN = 1040
TF_FFN = 256
TM_PROJ = 2080
TN_PROJ = 512
TM_OUT = 1040
TK_OUT = 512
TT_S5 = 256
TT_HGRN = 256
TT_MLPRE = 512
LC_MLPRE = 512
L_ML = 256
S5_CHUNKS = 8


def kernel(x_prompt, x_sample, state_s5_re, state_s5_im, state_hgrn, state_conv, state_mlstm_c, state_mlstm_n, state_mlstm_m, norm_gain, ffn_w_gate, ffn_w_up, ffn_w_down, ab_w_in, ab_w_out, s5_lambda_re, s5_lambda_im, s5_log_step, s5_b_re, s5_b_im, s5_c_re, s5_c_im, s5_d, s5_glu_w, s5_glu_b, hgrn_lb_logits, hgrn_norm_gain, ml_w_up, ml_conv_w, ml_conv_b, ml_w_q, ml_w_k, ml_w_v, ml_w_gates, ml_b_gates, ml_norm_gain, ml_skip, ml_w_down):
    bp, t, d = x_prompt.shape
    bs = x_sample.shape[0]
    depth = norm_gain.shape[0]
    mp = bp * t
    n_ab, g5, p5 = s5_lambda_re.shape
    n_state = g5 * p5
    s5_w = g5 * S5_GROUP
    hg_w = HG_HEADS * LANE
    inner = ml_conv_w.shape[-1]
    n_c = ml_conv_w.shape[0]
    dh = inner // ML_HEADS

    gains = norm_gain.reshape(depth, norm_gain.shape[1], 1, d)
    h = jnp.concatenate([x_prompt.reshape(mp, d), x_sample.reshape(bs, d)], axis=0)

    out = dict(p_s5r=[], p_s5i=[], p_hg=[], p_conv=[], p_c=[], p_n=[], p_m=[],
               s_s5r=[], s_s5i=[], s_conv=[], s_n=[], s_m=[])
    s_hg = s_c = None

    for layer in range(depth):
        j = layer // 2
        h = _ffn(h, gains, ffn_w_gate, ffn_w_up, ffn_w_down, layer, 0, 0, 1, TM_FFN, TF_FFN)
        if layer % 2 == 0:
            proj = _norm_matmul(h, gains, layer, 2, ab_w_in, j, TM_PROJ, TN_PROJ)
            prm = _s5_params(s5_lambda_re[j].reshape(1, n_state), s5_lambda_im[j].reshape(1, n_state),
                             jnp.repeat(s5_log_step[j], p5).reshape(1, n_state),
                             s5_b_re[j].transpose(2, 0, 1).reshape(S5_GROUP, n_state),
                             s5_b_im[j].transpose(2, 0, 1).reshape(S5_GROUP, n_state), 8)
            ap2r, ap2i, tabr, tabi, bbt_r, bbt_i = prm
            bbr, bbi, ccr, cci = _s5_layout(bbt_r, bbt_i, s5_c_re[j], s5_c_im[j], S5_CHUNKS)
            s5p = (ap2r, ap2i, tabr, tabi, bbr, bbi, ccr, cci)
            d3 = s5_d.reshape(n_ab, 1, s5_w)
            gb3 = s5_glu_b.reshape(n_ab, 1, s5_w)
            ya_p, sr_p, si_p = _s5_seq(proj, 0, bp, t, TT_S5, j, s5p, d3, s5_glu_w, gb3)
            ya_s, sr_s, si_s = _s5_step(proj, mp, bs, j, s5p, d3, s5_glu_w, gb3,
                                        state_s5_re[j].reshape(bs, n_state), state_s5_im[j].reshape(bs, n_state))
            out["p_s5r"].append(sr_p.reshape(bp, g5, p5))
            out["p_s5i"].append(si_p.reshape(bp, g5, p5))
            out["s_s5r"].append(sr_s.reshape(bs, g5, p5))
            out["s_s5i"].append(si_s.reshape(bs, g5, p5))
            hgain3 = hgrn_norm_gain.reshape(n_ab, 1, hg_w)
            yb_p, hg_p = _hgrn_seq(proj, 0, bp, t, TT_HGRN, j, s5_w, hgrn_lb_logits, hgain3)
            proj_s = proj[mp:]
            bb_h = 8
            yb_s, hg_s = _hgrn_step(
                proj_s.reshape(bs, 1, proj.shape[1]),
                (_to_cols(proj_s[:, s5_w:s5_w + hg_w], HG_HEADS, bb_h),
                 _to_cols(proj_s[:, s5_w + hg_w:s5_w + 2 * hg_w], HG_HEADS, bb_h)),
                j, s5_w, hgrn_lb_logits, hgrn_lb_logits.T, hgain3, state_hgrn, s_hg, bb_h)
            s_hg = hg_s
            out["p_hg"].append(hg_p)
            ya = jnp.concatenate([ya_p, ya_s], axis=0)
            yb = jnp.concatenate([yb_p, yb_s], axis=0)
            h = _matmul_postnorm([ya, yb], ab_w_out, j, h, gains, layer, 3, TM_OUT, TK_OUT)
        else:
            up = _norm_matmul(h, gains, layer, 2, ml_w_up, j, TM_PROJ, TN_PROJ)
            bd = (_block_diag(ml_w_q[j], MXU_DIM)[None], _block_diag(ml_w_k[j], MXU_DIM)[None],
                  _block_diag(ml_w_v[j], MXU_DIM)[None])
            wg4 = ml_w_gates[j].reshape(1, 3, inner, 2 * ML_HEADS)
            bg3 = ml_b_gates[j].reshape(1, 1, 2 * ML_HEADS)
            cw = ml_conv_w[j][None]
            cb3 = ml_conv_b[j].reshape(1, 1, inner)
            gain3 = ml_norm_gain.reshape(n_c, 1, inner)
            skip3 = ml_skip.reshape(n_c, 1, inner)
            q_p, k_p, v_p, xc_p, g_p = _ml_pre(up, 0, bp, t, TT_MLPRE, LC_MLPRE, 0, cw, cb3, bd, wg4, bg3)
            hf_p, c_p, n_p, m_p = _ml_seq(q_p, k_p, v_p, xc_p, up, 0, g_p, g_p.T, bp, t, L_ML, j, gain3, skip3)
            out["p_conv"].append(up[:mp, :inner].reshape(bp, t, inner)[:, t - (ML_CONV - 1):])
            out["p_c"].append(c_p)
            out["p_n"].append(n_p.reshape(bp, ML_HEADS, dh))
            out["p_m"].append(m_p.reshape(bp, ML_HEADS))
            taps = state_conv[j].transpose(1, 0, 2)
            q_s, k_s, v_s, xc_s, g_s = _ml_pre(up, mp, bs, 1, None, LC_MLPRE, 0, cw, cb3, bd, wg4, bg3, taps=taps)
            bb_m = 4
            sc = lambda a: a.T.reshape(ML_HEADS, bs // bb_m, 1, bb_m)
            r3 = lambda a: a.reshape(bs, 1, a.shape[-1])
            hf_s, c_s, n_s, m_s = _ml_step(
                r3(q_s), r3(k_s), r3(v_s), r3(xc_s), r3(up[mp:]),
                _to_cols(q_s, ML_HEADS, bb_m), _to_cols(k_s, ML_HEADS, bb_m),
                sc(g_s[:, :ML_HEADS]), sc(g_s[:, ML_HEADS:]), sc(state_mlstm_m[j]),
                j, gain3, skip3, state_mlstm_c, state_mlstm_n.reshape(n_c, bs, ML_HEADS, 1, dh), s_c, bb_m)
            s_c = c_s
            out["s_conv"].append(jnp.concatenate([state_conv[j][:, 1:], up[mp:, None, :inner]], axis=1))
            out["s_n"].append(n_s.reshape(bs, ML_HEADS, dh))
            out["s_m"].append(m_s.reshape(ML_HEADS, bs).T)
            hf = jnp.concatenate([hf_p, hf_s.reshape(bs, inner)], axis=0)
            h = _matmul_postnorm([hf], ml_w_down, j, h, gains, layer, 3, TM_OUT, TK_OUT)
        h = _ffn(h, gains, ffn_w_gate, ffn_w_up, ffn_w_down, layer, 1, 4, 5, TM_FFN, TF_FFN)

    st = lambda name: jnp.stack(out[name])
    return (h[:mp].reshape(bp, t, d), h[mp:].reshape(bs, 1, d),
            st("p_s5r"), st("p_s5i"), st("p_hg"), st("p_conv"), st("p_c"), st("p_n"), st("p_m"),
            st("s_s5r"), st("s_s5i"), s_hg, st("s_conv"), s_c, st("s_n"), st("s_m"))
```

```python
import functools
import math

import jax
import jax.numpy as jnp
from jax import lax
from jax.experimental import pallas as pl
from jax.experimental.pallas import tpu as pltpu

F32 = jnp.float32
BF16 = jnp.bfloat16
EPS = 1e-6
MIB = 1024 * 1024

S5_GROUP = 16
S5_STATE = 64
HG_HEADS = 8
ML_HEADS = 8
ML_CONV = 4
ML_BLOCK = 4

LANE = 128
MXU_DIM = 256


def _dot(a, b):
    return jnp.dot(a.astype(BF16), b.astype(BF16), preferred_element_type=F32)


def _dot_nt(a, b):
    return lax.dot_general(a.astype(BF16), b.astype(BF16), (((1,), (1,)), ((), ())),
                           preferred_element_type=F32)


def _dot_tn(a, b):
    return lax.dot_general(a.astype(BF16), b.astype(BF16), (((0,), (0,)), ((), ())),
                           preferred_element_type=F32)


def _sigmoid(x):
    return 1.0 / (1.0 + jnp.exp(-x))


def _silu(x):
    return x * _sigmoid(x)


def _log_sigmoid(x):
    return jnp.minimum(x, 0.0) - jnp.log1p(jnp.exp(-jnp.abs(x)))


def _logaddexp(a, b):
    return jnp.maximum(a, b) + jnp.log1p(jnp.exp(-jnp.abs(a - b)))


def _gelu_tanh(x):
    return x * (0.5 * (1.0 + jnp.tanh(math.sqrt(2.0 / math.pi) * (x + 0.044715 * (x * x * x)))))


def _rms(x, g):
    return x * lax.rsqrt(jnp.mean(x * x, axis=-1, keepdims=True) + EPS) * g


def _shift_rows(x, s):
    n = x.shape[0]
    if s % 8 == 0:
        return jnp.concatenate([jnp.zeros((s,) + x.shape[1:], x.dtype), x[:n - s]], axis=0)
    row = lax.broadcasted_iota(jnp.int32, x.shape, 0)
    return jnp.where(row >= s, pltpu.roll(x, s, 0), 0.0)


def _shared_out(prev, in_specs, args, out_index):
    if prev is None:
        return {}
    in_specs.append(pl.BlockSpec(memory_space=pl.ANY))
    args.append(prev)
    return {len(args) - 1: out_index}


def _params(n_axes, vmem_mib):
    return pltpu.CompilerParams(dimension_semantics=("arbitrary",) * n_axes,
                                vmem_limit_bytes=vmem_mib * MIB)


def _ffn_body(has_tail, *refs):
    if has_tail:
        (x_ref, gpre_ref, gpost_ref, wg_ref, wu_ref, wd_ref, wgt_ref, wut_ref, wdt_ref,
         o_ref, xn_ref, h_ref) = refs
    else:
        (x_ref, gpre_ref, gpost_ref, wg_ref, wu_ref, wd_ref, o_ref, xn_ref, h_ref) = refs
    j = pl.program_id(1)
    nj = pl.num_programs(1) - 1

    @pl.when(j == 0)
    def _():
        xn_ref[...] = _rms(x_ref[...], gpre_ref[...]).astype(BF16)
        o_ref[...] = jnp.zeros_like(o_ref)
        h_ref[...] = jnp.zeros_like(h_ref)

    def hidden(wg, wu):
        xn = xn_ref[...]
        return (_silu(_dot(xn, wg)) * _dot(xn, wu)).astype(BF16)

    @pl.when(j < nj)
    def _():
        o_ref[...] += _dot(h_ref[...], wd_ref[...])
        h_ref[...] = hidden(wg_ref[...], wu_ref[...])

    @pl.when(j == nj)
    def _():
        acc = o_ref[...] + _dot(h_ref[...], wd_ref[...])
        if has_tail:
            wt = jnp.concatenate([wgt_ref[...].astype(BF16), wut_ref[...].astype(BF16)], axis=-1)
            gu = _dot(xn_ref[...], wt)
            ht = (_silu(gu[:, :LANE]) * gu[:, LANE:]).astype(BF16)
            acc = acc + _dot(ht, wdt_ref[...])
        o_ref[...] = x_ref[...] + 0.5 * _rms(acc, gpost_ref[...])


def _ffn(h, gains, wg, wu, wd, layer, idx, k_pre, k_post, tm, tf):
    m, d = h.shape
    dff = wg.shape[-1]
    nj = dff // tf
    tail = dff - nj * tf
    assert m % tm == 0 and tail in (0, LANE) and nj >= 1
    tail_blk = dff // LANE - 1
    one = pl.Buffered(1)
    in_specs = [
        pl.BlockSpec((tm, d), lambda i, j: (i, 0), pipeline_mode=one),
        pl.BlockSpec((None, None, 1, d), lambda i, j: (layer, k_pre, 0, 0)),
        pl.BlockSpec((None, None, 1, d), lambda i, j: (layer, k_post, 0, 0)),
        pl.BlockSpec((None, None, d, tf), lambda i, j: (layer, idx, 0, jnp.minimum(j, nj - 1))),
        pl.BlockSpec((None, None, d, tf), lambda i, j: (layer, idx, 0, jnp.minimum(j, nj - 1))),
        pl.BlockSpec((None, None, tf, d), lambda i, j: (layer, idx, jnp.maximum(j - 1, 0), 0)),
    ]
    args = [h, gains, gains, wg, wu, wd]
    if tail:
        in_specs += [
            pl.BlockSpec((None, None, d, LANE), lambda i, j: (layer, idx, 0, tail_blk), pipeline_mode=one),
            pl.BlockSpec((None, None, d, LANE), lambda i, j: (layer, idx, 0, tail_blk), pipeline_mode=one),
            pl.BlockSpec((None, None, LANE, d), lambda i, j: (layer, idx, tail_blk, 0), pipeline_mode=one),
        ]
        args += [wg, wu, wd]
    return pl.pallas_call(
        functools.partial(_ffn_body, bool(tail)),
        grid=(m // tm, nj + 1),
        in_specs=in_specs,
        out_specs=pl.BlockSpec((tm, d), lambda i, j: (i, 0)),
        out_shape=jax.ShapeDtypeStruct((m, d), F32),
        scratch_shapes=[pltpu.VMEM((tm, d), BF16), pltpu.VMEM((tm, tf), BF16)],
        compiler_params=_params(2, 58),
        name="ffn",
    )(*args)


def _nmm_body(x_ref, g_ref, w_ref, o_ref, xn_ref):
    @pl.when(pl.program_id(1) == 0)
    def _():
        xn_ref[...] = _rms(x_ref[...], g_ref[...]).astype(BF16)

    o_ref[...] = _dot(xn_ref[...], w_ref[...])


def _norm_matmul(h, gains, layer, k_gain, w, j_w, tm, tn):
    m, d = h.shape
    n = w.shape[-1]
    assert m % tm == 0 and n % tn == 0
    return pl.pallas_call(
        _nmm_body,
        grid=(m // tm, n // tn),
        in_specs=[
            pl.BlockSpec((tm, d), lambda i, j: (i, 0), pipeline_mode=pl.Buffered(1)),
            pl.BlockSpec((None, None, 1, d), lambda i, j: (layer, k_gain, 0, 0)),
            pl.BlockSpec((None, d, tn), lambda i, j: (j_w, 0, j)),
        ],
        out_specs=pl.BlockSpec((tm, tn), lambda i, j: (i, j)),
        out_shape=jax.ShapeDtypeStruct((m, n), F32),
        scratch_shapes=[pltpu.VMEM((tm, d), BF16)],
        compiler_params=_params(2, 48),
        name="norm_matmul",
    )(h, gains, w)


def _mmpn_body(steps, *refs):
    n_in = len(steps)
    y_refs = refs[:n_in]
    w_ref, h_ref, g_ref, o_ref = refs[n_in:]
    k = pl.program_id(1)
    last = pl.num_programs(1) - 1

    @pl.when(k == 0)
    def _():
        o_ref[...] = jnp.zeros_like(o_ref)

    s0 = 0
    for y_ref, ns in zip(y_refs, steps):
        @pl.when((k >= s0) & (k < s0 + ns))
        def _(y_ref=y_ref):
            o_ref[...] += _dot(y_ref[...], w_ref[...])
        s0 += ns

    @pl.when(k == last)
    def _():
        o_ref[...] = h_ref[...] + _rms(o_ref[...], g_ref[...])


def _matmul_postnorm(ys, w, j_w, h, gains, layer, k_gain, tm, tk):
    m, d = h.shape
    steps = tuple(y.shape[1] // tk for y in ys)
    assert all(y.shape[1] % tk == 0 for y in ys) and sum(steps) * tk == w.shape[1] and m % tm == 0
    in_specs = []
    s0 = 0
    for ns in steps:
        in_specs.append(pl.BlockSpec(
            (tm, tk), lambda i, k, s0=s0, ns=ns: (i, jnp.clip(k - s0, 0, ns - 1))))
        s0 += ns
    in_specs += [
        pl.BlockSpec((None, tk, d), lambda i, k: (j_w, k, 0)),
        pl.BlockSpec((tm, d), lambda i, k: (i, 0), pipeline_mode=pl.Buffered(1)),
        pl.BlockSpec((None, None, 1, d), lambda i, k: (layer, k_gain, 0, 0)),
    ]
    return pl.pallas_call(
        functools.partial(_mmpn_body, steps),
        grid=(m // tm, sum(steps)),
        in_specs=in_specs,
        out_specs=pl.BlockSpec((tm, d), lambda i, k: (i, 0)),
        out_shape=jax.ShapeDtypeStruct((m, d), F32),
        compiler_params=_params(2, 56),
        name="matmul_postnorm",
    )(*ys, w, h, gains)


def _s5_param_body(lre_ref, lim_ref, ls_ref, br_ref, bi_ref,
                   ap2r_ref, ap2i_ref, tabr_ref, tabi_ref, bbr_ref, bbi_ref):
    lr = jnp.minimum(lre_ref[...], -1e-4)
    li = lim_ref[...]
    step = jnp.exp(ls_ref[...])
    mag = jnp.exp(lr * step)
    ang = li * step
    ar = mag * jnp.cos(ang)
    ai = mag * jnp.sin(ang)
    den = lr * lr + li * li
    nr = ar - 1.0
    gr = (nr * lr + ai * li) / den
    gi = (ai * lr - nr * li) / den
    br = br_ref[...]
    bi = bi_ref[...]
    bbr_ref[...] = gr * br - gi * bi
    bbi_ref[...] = gr * bi + gi * br
    row = lax.broadcasted_iota(jnp.int32, (8, ar.shape[1]), 0)
    pr, pi = ar, ai
    tabr_ref[0:1, :] = ar
    tabi_ref[0:1, :] = ai
    for k in range(ap2r_ref.shape[0]):
        ap2r_ref[k:k + 1, :] = pr
        ap2i_ref[k:k + 1, :] = pi
        n = 2 ** k
        if n < 8:
            tr = tabr_ref[0:n, :]
            ti = tabi_ref[0:n, :]
            tabr_ref[n:2 * n, :] = tr * pr - ti * pi
            tabi_ref[n:2 * n, :] = tr * pi + ti * pr
            tabr_ref[8 * (k + 1):8 * (k + 2), :] = jnp.where(row >= n, pr, 0.0)
            tabi_ref[8 * (k + 1):8 * (k + 2), :] = jnp.where(row >= n, pi, 0.0)
        pr, pi = pr * pr - pi * pi, pr * pi + pi * pr


S5_TAB_ROWS = 32


def _s5_params(lam_re, lam_im, log_step, b_re, b_im):
    n = lam_re.shape[-1]
    c = b_re.shape[0]
    tt = S5_TAB_ROWS
    full = lambda shape: pl.BlockSpec(shape, lambda: (0,) * len(shape))
    return pl.pallas_call(
        _s5_param_body,
        in_specs=[full((1, n))] * 3 + [full((c, n))] * 2,
        out_specs=[full((8, n))] * 2 + [full((tt, n))] * 2 + [full((c, n))] * 2,
        out_shape=[jax.ShapeDtypeStruct((8, n), F32)] * 2 + [jax.ShapeDtypeStruct((tt, n), F32)] * 2
        + [jax.ShapeDtypeStruct((c, n), F32)] * 2,
        compiler_params=pltpu.CompilerParams(vmem_limit_bytes=32 * MIB),
        name="s5_params",
    )(lam_re, lam_im, log_step, b_re, b_im)


def _s5_body(scan, n_chunks, *refs):
    if scan:
        (u_ref, ap2r_ref, ap2i_ref, tabr_ref, tabi_ref, bbr_ref, bbi_ref, ccr_ref, cci_ref, d_ref,
         gw_ref, gb_ref, y_ref, sr_ref, si_ref, cr_s, ci_s) = refs
    else:
        (u_ref, ap2r_ref, ap2i_ref, bbr_ref, bbi_ref, ccr_ref, cci_ref, d_ref,
         gw_ref, gb_ref, h0r_ref, h0i_ref) = refs[:12]
        y_ref, sr_ref, si_ref = refs[-3:]
    tt = u_ref.shape[0]
    cw = u_ref.shape[1] // n_chunks
    sw = bbr_ref.shape[2]

    if scan:
        @pl.when(pl.program_id(1) == 0)
        def _():
            cr_s[...] = jnp.zeros_like(cr_s)
            ci_s[...] = jnp.zeros_like(ci_s)

    u = u_ref[...]
    ys = []
    for c in range(n_chunks):
        uc = u[:, c * cw:(c + 1) * cw]
        lanes = slice(c * sw, (c + 1) * sw)
        xr = _dot(uc, bbr_ref[c])
        xi = _dot(uc, bbi_ref[c])
        if scan:
            xr = xr.reshape(tt // 8, 8, sw)
            xi = xi.reshape(tt // 8, 8, sw)
            for k in range(3):
                pr = tabr_ref[8 * (k + 1):8 * (k + 2), lanes]
                pi = tabi_ref[8 * (k + 1):8 * (k + 2), lanes]
                sr = pltpu.roll(xr, 2 ** k, 1)
                si = pltpu.roll(xi, 2 ** k, 1)
                xr, xi = xr + (pr * sr - pi * si), xi + (pr * si + pi * sr)
            tr = tabr_ref[0:8, lanes]
            ti = tabi_ref[0:8, lanes]
            hr = cr_s[:, lanes]
            hi = ci_s[:, lanes]
            out_r, out_i = [], []
            for g in range(tt // 8):
                gr = xr[g] + (tr * hr - ti * hi)
                gi = xi[g] + (tr * hi + ti * hr)
                out_r.append(gr)
                out_i.append(gi)
                hr, hi = gr[7:8], gi[7:8]
            xr = jnp.concatenate(out_r, axis=0)
            xi = jnp.concatenate(out_i, axis=0)
            cr_s[:, lanes] = hr
            ci_s[:, lanes] = hi
        else:
            tr = ap2r_ref[0:1, lanes]
            ti = ap2i_ref[0:1, lanes]
            hr = h0r_ref[:, lanes]
            hi = h0i_ref[:, lanes]
            xr, xi = xr + (tr * hr - ti * hi), xi + (tr * hi + ti * hr)
            sr_ref[:, lanes] = xr
            si_ref[:, lanes] = xi
        ys.append(_dot(xr, ccr_ref[c]) - _dot(xi, cci_ref[c]) + d_ref[:, c * cw:(c + 1) * cw] * uc)
    y = _gelu_tanh(jnp.concatenate(ys, axis=-1))
    y_ref[...] = y * _sigmoid(_dot(y, gw_ref[...]) + gb_ref[...])
    if scan:
        sr_ref[0] = cr_s[...]
        si_ref[0] = ci_s[...]


def _s5_specs(j, width, n_chunks, cw, sw, n_state, nidx):
    zero = (0,) * nidx
    return dict(
        ap2=pl.BlockSpec((8, n_state), lambda *a: (0, 0)),
        bb=pl.BlockSpec((n_chunks, cw, sw), lambda *a: (0, 0, 0)),
        cc=pl.BlockSpec((n_chunks, sw, cw), lambda *a: (0, 0, 0)),
        d=pl.BlockSpec((None, 1, width), lambda *a: (j, 0, 0)),
        gw=pl.BlockSpec((None, width, width), lambda *a: (j, 0, 0)),
        gb=pl.BlockSpec((None, 1, width), lambda *a: (j, 0, 0)),
    )


def _s5_seq(proj, row0, bsz, t, tt, j, prm, d3, glu_w, glu_b3):
    ap2r, ap2i, tabr, tabi, bbr, bbi, ccr, cci = prm
    n_chunks, cw, sw = bbr.shape
    width = n_chunks * cw
    n_state = n_chunks * sw
    nt = t // tt
    rb0 = row0 // tt
    sp = _s5_specs(j, width, n_chunks, cw, sw, n_state, 2)
    tab = pl.BlockSpec((S5_TAB_ROWS, n_state), lambda b, i: (0, 0))
    st = pl.BlockSpec((1, 1, n_state), lambda b, i: (b, 0, 0))
    y, sr, si = pl.pallas_call(
        functools.partial(_s5_body, True, n_chunks),
        grid=(bsz, nt),
        in_specs=[pl.BlockSpec((tt, width), lambda b, i: (rb0 + b * nt + i, 0)),
                  sp["ap2"], sp["ap2"], tab, tab, sp["bb"], sp["bb"], sp["cc"], sp["cc"],
                  sp["d"], sp["gw"], sp["gb"]],
        out_specs=[pl.BlockSpec((tt, width), lambda b, i: (b * nt + i, 0)), st, st],
        out_shape=[jax.ShapeDtypeStruct((proj.shape[0], width), F32),
                   jax.ShapeDtypeStruct((bsz, 1, n_state), F32),
                   jax.ShapeDtypeStruct((bsz, 1, n_state), F32)],
        scratch_shapes=[pltpu.VMEM((1, n_state), F32)] * 2,
        compiler_params=_params(2, 48),
        name="s5_seq",
    )(proj, ap2r, ap2i, tabr, tabi, bbr, bbi, ccr, cci, d3, glu_w, glu_b3)
    return y, sr[:, 0], si[:, 0]


def _s5_step(proj, row0, bsz, j, prm, d3, glu_w, glu_b3, h0r, h0i, y_all):
    ap2r, ap2i, _, _, bbr, bbi, ccr, cci = prm
    n_chunks, cw, sw = bbr.shape
    width = n_chunks * cw
    n_state = n_chunks * sw
    rb0 = row0 // bsz
    sp = _s5_specs(j, width, n_chunks, cw, sw, n_state, 1)
    st = pl.BlockSpec((bsz, n_state), lambda i: (0, 0))
    in_specs = [pl.BlockSpec((bsz, width), lambda i: (rb0, 0)),
                sp["ap2"], sp["ap2"], sp["bb"], sp["bb"], sp["cc"], sp["cc"],
                sp["d"], sp["gw"], sp["gb"], st, st]
    args = [proj, ap2r, ap2i, bbr, bbi, ccr, cci, d3, glu_w, glu_b3, h0r, h0i]
    aliases = _shared_out(y_all, in_specs, args, 0)
    return pl.pallas_call(
        functools.partial(_s5_body, False, n_chunks),
        grid=(1,),
        in_specs=in_specs,
        out_specs=[pl.BlockSpec((bsz, width), lambda i: (rb0, 0)), st, st],
        out_shape=[jax.ShapeDtypeStruct(y_all.shape, F32),
                   jax.ShapeDtypeStruct((bsz, n_state), F32),
                   jax.ShapeDtypeStruct((bsz, n_state), F32)],
        input_output_aliases=aliases,
        compiler_params=_params(1, 48),
        name="s5_step",
    )(*args)


def _hgrn_lower_bound(logits, j, axis):
    n = logits.shape[axis]
    mx = jnp.max(logits, axis=axis, keepdims=True)
    e = jnp.exp(logits - mx)
    sm = e / jnp.sum(e, axis=axis, keepdims=True)
    pick = (lambda i: sm[i:i + 1, :]) if axis == 0 else (lambda i: sm[:, i:i + 1])
    cum = pick(0)
    for i in range(1, j + 1):
        cum = cum + pick(i)
    assert j < n
    return cum - pick(0)


def _hgrn_gates(fl, lb):
    logf = _logaddexp(_log_sigmoid(fl), jnp.log(lb) + _log_sigmoid(-fl))
    k = (1.0 - lb) * _sigmoid(-fl)
    return logf, k


def _hgrn_seq_body(j, chunk, sub, q_ref, f_ref, v_ref, gt_ref, lg_ref, gain_ref,
                   y_ref, s_ref, st_s, q_s, k_s, v_s, b_s, o_s):
    tt, dk = q_ref.shape
    i = pl.program_id(2)

    @pl.when(i == 0)
    def _():
        st_s[...] = jnp.zeros_like(st_s)

    lb = _hgrn_lower_bound(lg_ref[...], j, 0)
    logf, k = _hgrn_gates(f_ref[...], lb)
    rin = lax.broadcasted_iota(jnp.int32, (tt, dk), 0) & (chunk - 1)
    b = logf
    s = 1
    while s < chunk:
        b = b + jnp.where(rin >= s, _shift_rows(b, s), 0.0)
        s *= 2
    q_s[...] = _silu(q_ref[...])
    k_s[...] = k
    v_s[...] = v_ref[...]
    b_s[...] = b
    nsub = chunk // sub
    tsub = lax.broadcasted_iota(jnp.int32, (sub, dk), 0)

    def one_chunk(c, carry):
        r0 = pl.multiple_of(c * chunk, chunk)
        qc = q_s[pl.ds(r0, chunk), :]
        kc = k_s[pl.ds(r0, chunk), :]
        vc = v_s[pl.ds(r0, chunk), :]
        bc = b_s[pl.ds(r0, chunk), :]
        acc = [jnp.zeros((sub, dk), F32) for _ in range(nsub)]
        for s_i in range(chunk):
            ks = kc[s_i:s_i + 1]
            bs = bc[s_i:s_i + 1]
            vs = vc[s_i:s_i + 1]
            sb = s_i // sub
            for tb in range(sb, nsub):
                diff = bc[tb * sub:(tb + 1) * sub] - bs
                if tb == sb:
                    diff = jnp.where(tsub >= s_i - sb * sub, diff, -jnp.inf)
                p = qc[tb * sub:(tb + 1) * sub] * ks * jnp.exp(diff)
                acc[tb] = acc[tb] + jnp.sum(p, axis=-1, keepdims=True) * vs
        st = st_s[...]
        o = jnp.concatenate(acc, axis=0) + _dot_nt(qc * jnp.exp(bc), st)
        b_last = bc[chunk - 1:chunk]
        st_s[...] = st * jnp.exp(b_last) + _dot_tn(vc, kc * jnp.exp(b_last - bc))
        o_s[pl.ds(r0, chunk), :] = o
        return carry

    lax.fori_loop(0, tt // chunk, one_chunk, 0, unroll=2)
    o = o_s[...]
    o = o * lax.rsqrt(jnp.mean(o * o, axis=-1, keepdims=True) + EPS)
    y_ref[...] = o * gain_ref[...] * _silu(gt_ref[...])

    @pl.when(i == pl.num_programs(2) - 1)
    def _():
        s_ref[...] = st_s[...].T


def _hgrn_seq(proj, row0, bsz, t, tt, j, col0, logits, gain3, chunk=32, sub=8):
    h, dk = HG_HEADS, LANE
    nt = t // tt
    rb0 = row0 // tt
    cb = col0 // dk
    sec = lambda n: pl.BlockSpec((tt, dk), lambda b, hh, i: (rb0 + b * nt + i, cb + n * h + hh))
    y, s = pl.pallas_call(
        functools.partial(_hgrn_seq_body, j, chunk, sub),
        grid=(bsz, h, nt),
        in_specs=[sec(0), sec(1), sec(2), sec(3),
                  pl.BlockSpec((logits.shape[0], dk), lambda b, hh, i: (0, hh)),
                  pl.BlockSpec((None, 1, dk), lambda b, hh, i: (j, 0, hh))],
        out_specs=[pl.BlockSpec((tt, dk), lambda b, hh, i: (b * nt + i, hh)),
                   pl.BlockSpec((None, None, dk, dk), lambda b, hh, i: (b, hh, 0, 0))],
        out_shape=[jax.ShapeDtypeStruct((proj.shape[0], h * dk), F32),
                   jax.ShapeDtypeStruct((bsz, h, dk, dk), F32)],
        scratch_shapes=[pltpu.VMEM((dk, dk), F32)] + [pltpu.VMEM((tt, dk), F32)] * 5,
        compiler_params=_params(3, 32),
        name="hgrn_seq",
    )(proj, proj, proj, proj, logits, gain3)
    return y, s


def _hgrn_step_body(j, bb, q_ref, f_ref, v_ref, gt_ref, qc_ref, fc_ref, lg_ref, lgt_ref, gain_ref,
                    s_ref, *rest):
    y_ref, so_ref = rest[-2:]
    lb_r = _hgrn_lower_bound(lg_ref[...], j, 0)
    lb_c = _hgrn_lower_bound(lgt_ref[...], j, 1)
    qcol = _silu(qc_ref[...])
    logf_c, kcol = _hgrn_gates(fc_ref[...], lb_c)
    fcol = jnp.exp(logf_c)
    for n in range(bb):
        s = s_ref[n]
        q_r = _silu(q_ref[n])
        _, k_r = _hgrn_gates(f_ref[n], lb_r)
        v_r = v_ref[n]
        q_c = qcol[:, n:n + 1]
        f_c = fcol[:, n:n + 1]
        k_c = kcol[:, n:n + 1]
        score = jnp.sum(q_r * k_r, axis=-1, keepdims=True)
        o = score * v_r + jnp.sum((q_c * f_c) * s, axis=0, keepdims=True)
        so_ref[n] = f_c * s + k_c * v_r
        o = o * lax.rsqrt(jnp.mean(o * o, axis=-1, keepdims=True) + EPS)
        y_ref[n:n + 1, :] = o * gain_ref[...] * _silu(gt_ref[n])


def _hgrn_step(proj3, proj_t, j, col0, logits, logits_t, gain3, s_all, s_prev, y_all, row0, bb=8):
    bsz = proj3.shape[0]
    h, dk = HG_HEADS, LANE
    cb = col0 // dk
    qt, ft = proj_t
    row = lambda n: pl.BlockSpec((bb, 1, dk), lambda hh, i: (i, 0, cb + n * h + hh))
    col = pl.BlockSpec((None, None, dk, bb), lambda hh, i: (hh, i, 0, 0))
    st = pl.BlockSpec((None, bb, None, dk, dk), lambda hh, i: (j, i, hh, 0, 0))
    in_specs = [row(0), row(1), row(2), row(3), col, col,
                pl.BlockSpec((logits.shape[0], dk), lambda hh, i: (0, hh)),
                pl.BlockSpec((dk, logits.shape[0]), lambda hh, i: (hh, 0)),
                pl.BlockSpec((None, 1, dk), lambda hh, i: (j, 0, hh)),
                st]
    args = [proj3, proj3, proj3, proj3, qt, ft, logits, logits_t, gain3, s_all]
    aliases = _shared_out(y_all, in_specs, args, 0)
    aliases.update(_shared_out(s_prev, in_specs, args, 1))
    rb0 = row0 // bb
    return pl.pallas_call(
        functools.partial(_hgrn_step_body, j, bb),
        grid=(h, bsz // bb),
        in_specs=in_specs,
        out_specs=[pl.BlockSpec((bb, dk), lambda hh, i: (rb0 + i, hh)), st],
        out_shape=[jax.ShapeDtypeStruct(y_all.shape, F32),
                   jax.ShapeDtypeStruct(s_all.shape, F32)],
        input_output_aliases=aliases,
        compiler_params=_params(2, 32),
        name="hgrn_step",
    )(*args)


def _ml_qkv_gates(c, xc, xm, bdq_ref, bdk_ref, bdv_ref, wg_ref, bg_ref, q_ref, k_ref, v_ref, xc_ref, g_ref):
    nblk = bdq_ref.shape[0]
    w = bdq_ref.shape[1]
    xcb = xc.astype(BF16)
    xmb = xm.astype(BF16)
    q = jnp.concatenate([_dot(xcb[:, s * w:(s + 1) * w], bdq_ref[s]) for s in range(nblk)], axis=-1)
    k = jnp.concatenate([_dot(xcb[:, s * w:(s + 1) * w], bdk_ref[s]) for s in range(nblk)], axis=-1)
    v = jnp.concatenate([_dot(xmb[:, s * w:(s + 1) * w], bdv_ref[s]) for s in range(nblk)], axis=-1)
    q_ref[...] = q
    k_ref[...] = k
    v_ref[...] = v
    xc_ref[...] = xc
    g = _dot(q, wg_ref[0]) + _dot(k, wg_ref[1]) + _dot(v, wg_ref[2])

    @pl.when(c == 0)
    def _():
        g_ref[...] = g + bg_ref[...]

    @pl.when(c > 0)
    def _():
        g_ref[...] += g


def _ml_pre_seq_body(xm_ref, cw_ref, cb_ref, bdq_ref, bdk_ref, bdv_ref, wg_ref, bg_ref,
                     q_ref, k_ref, v_ref, xc_ref, g_ref, stage_s, halo_s):
    i = pl.program_id(1)
    c = pl.program_id(2)
    tt = xm_ref.shape[0]

    @pl.when(i == 0)
    def _():
        halo_s[c] = jnp.zeros(halo_s.shape[1:], F32)

    xm = xm_ref[...]
    stage_s[0:8, :] = halo_s[c]
    stage_s[8:, :] = xm
    halo_s[c] = xm[tt - 8:tt]
    acc = cb_ref[...] + cw_ref[ML_CONV - 1:ML_CONV, :] * xm
    for d in range(1, ML_CONV):
        acc = acc + cw_ref[ML_CONV - 1 - d:ML_CONV - d, :] * stage_s[8 - d:8 - d + tt, :]
    _ml_qkv_gates(c, _silu(acc), xm, bdq_ref, bdk_ref, bdv_ref, wg_ref, bg_ref,
                  q_ref, k_ref, v_ref, xc_ref, g_ref)


def _ml_pre_step_body(xm_ref, taps_ref, cw_ref, cb_ref, bdq_ref, bdk_ref, bdv_ref, wg_ref, bg_ref,
                      q_ref, k_ref, v_ref, xc_ref, g_ref):
    c = pl.program_id(0)
    xm = xm_ref[...]
    acc = cb_ref[...] + cw_ref[ML_CONV - 1:ML_CONV, :] * xm
    for d in range(1, ML_CONV):
        acc = acc + cw_ref[ML_CONV - 1 - d:ML_CONV - d, :] * taps_ref[ML_CONV - 1 - d]
    _ml_qkv_gates(c, _silu(acc), xm, bdq_ref, bdk_ref, bdv_ref, wg_ref, bg_ref,
                  q_ref, k_ref, v_ref, xc_ref, g_ref)


def _ml_pre(up, row0, bsz, t, tt, lc, j, conv_w, conv_b3, bd, wg4, bg3, taps=None):
    inner = conv_w.shape[-1]
    ng = wg4.shape[-1]
    nc = inner // lc
    nblk = lc // MXU_DIM
    bdq, bdk, bdv = bd
    seq = taps is None
    rows = bsz * t
    if seq:
        nt = t // tt
        rb0 = row0 // tt
        grid = (bsz, nt, nc)
        rmap = lambda b, i, c: b * nt + i
        cmap = lambda b, i, c: c
    else:
        tt = bsz
        rb0 = row0 // tt
        grid = (nc,)
        rmap = lambda c: 0
        cmap = lambda c: c
    in_specs = [pl.BlockSpec((tt, lc), lambda *a: (rb0 + rmap(*a), cmap(*a)))]
    args = [up]
    if not seq:
        in_specs.append(pl.BlockSpec((ML_CONV - 1, tt, lc), lambda *a: (0, 0, cmap(*a))))
        args.append(taps)
    in_specs += [
        pl.BlockSpec((None, ML_CONV, lc), lambda *a: (j, 0, cmap(*a))),
        pl.BlockSpec((None, 1, lc), lambda *a: (j, 0, cmap(*a))),
        pl.BlockSpec((None, nblk, MXU_DIM, MXU_DIM), lambda *a: (j, cmap(*a), 0, 0)),
        pl.BlockSpec((None, nblk, MXU_DIM, MXU_DIM), lambda *a: (j, cmap(*a), 0, 0)),
        pl.BlockSpec((None, nblk, MXU_DIM, MXU_DIM), lambda *a: (j, cmap(*a), 0, 0)),
        pl.BlockSpec((None, 3, lc, ng), lambda *a: (j, 0, cmap(*a), 0)),
        pl.BlockSpec((None, 1, ng), lambda *a: (j, 0, 0)),
    ]
    args += [conv_w, conv_b3, bdq, bdk, bdv, wg4, bg3]
    big = pl.BlockSpec((tt, lc), lambda *a: (rmap(*a), cmap(*a)))
    return pl.pallas_call(
        _ml_pre_seq_body if seq else _ml_pre_step_body,
        grid=grid,
        in_specs=in_specs,
        out_specs=[big, big, big, big, pl.BlockSpec((tt, ng), lambda *a: (rmap(*a), 0))],
        out_shape=[jax.ShapeDtypeStruct((rows, inner), F32)] * 4 + [jax.ShapeDtypeStruct((rows, ng), F32)],
        scratch_shapes=([pltpu.VMEM((tt + 8, lc), F32), pltpu.VMEM((nc, 8, lc), F32)] if seq else []),
        compiler_params=_params(len(grid), 48),
        name="ml_pre_seq" if seq else "ml_pre_step",
    )(*args)


def _ml_out(h, gain, skip, xc, z):
    mu = jnp.mean(h, axis=-1, keepdims=True)
    hc = h - mu
    hn = hc * lax.rsqrt(jnp.mean(hc * hc, axis=-1, keepdims=True) + EPS)
    return (hn * gain + skip * xc) * _silu(z)


def _ml_seq_body(q_ref, k_ref, v_ref, xc_ref, z_ref, g_ref, gt_ref, gain_ref, skip_ref,
                 o_ref, c_ref, n_ref, m_ref):
    hh = pl.program_id(1)
    i = pl.program_id(2)
    L, dh = q_ref.shape
    nh = g_ref.shape[1] // 2

    @pl.when(i == 0)
    def _():
        c_ref[...] = jnp.zeros_like(c_ref)
        n_ref[...] = jnp.zeros_like(n_ref)
        m_ref[...] = jnp.zeros_like(m_ref)

    g = g_ref[...]
    gt = gt_ref[...]
    lane = lax.broadcasted_iota(jnp.int32, g.shape, 1)
    sub = lax.broadcasted_iota(jnp.int32, gt.shape, 0)
    ig_c = jnp.sum(jnp.where(lane == hh, g, 0.0), axis=1, keepdims=True)
    lf_c = _log_sigmoid(jnp.sum(jnp.where(lane == nh + hh, g, 0.0), axis=1, keepdims=True))
    ig_r = jnp.sum(jnp.where(sub == hh, gt, 0.0), axis=0, keepdims=True)
    lf_r = _log_sigmoid(jnp.sum(jnp.where(sub == nh + hh, gt, 0.0), axis=0, keepdims=True))
    tr = lax.broadcasted_iota(jnp.int32, (L, L), 0)
    tc = lax.broadcasted_iota(jnp.int32, (L, L), 1)
    causal = tr >= tc
    b_c = jnp.sum(jnp.where(causal, lf_r, 0.0), axis=1, keepdims=True)
    b_r = jnp.sum(jnp.where(tr <= tc, lf_c, 0.0), axis=0, keepdims=True)
    m = m_ref[...]
    dlog = jnp.where(causal, b_c - b_r + ig_r, -jnp.inf)
    gsum = b_c + m
    mt = jnp.maximum(gsum, jnp.max(dlog, axis=1, keepdims=True))
    w = jnp.exp(dlog - mt)
    gi = jnp.exp(gsum - mt)
    q = q_ref[...]
    k = k_ref[...] * (dh ** -0.5)
    v = v_ref[...]
    c_old = c_ref[...]
    n_old = n_ref[...]
    qk = _dot_nt(q, k) * w
    num = _dot(qk, v) + gi * _dot(q, c_old)
    den = jnp.sum(qk, axis=1, keepdims=True) + gi * jnp.sum(q * n_old, axis=1, keepdims=True)
    h = num / jnp.maximum(jnp.abs(den), jnp.exp(-mt))
    m_new = mt[L - 1:L]
    b_last = b_c[L - 1:L]
    decay = jnp.exp(b_last + m - m_new)
    ks = k * jnp.exp(b_last - b_c + ig_c - m_new)
    c_ref[...] = decay * c_old + _dot_tn(ks, v)
    n_ref[...] = decay * n_old + jnp.sum(ks, axis=0, keepdims=True)
    m_ref[...] = m_new
    o_ref[...] = _ml_out(h, gain_ref[...], skip_ref[...], xc_ref[...], z_ref[...])


def _ml_seq(q, k, v, xc, up, row0, gates, gates_t, bsz, t, L, j, gain3, skip3):
    inner = q.shape[1]
    nh = ML_HEADS
    dh = inner // nh
    nt = t // L
    rb0 = row0 // L
    blk = pl.BlockSpec((L, dh), lambda b, hh, i: (b * nt + i, hh))
    par = pl.BlockSpec((None, 1, dh), lambda b, hh, i: (j, 0, hh))
    return pl.pallas_call(
        _ml_seq_body,
        grid=(bsz, nh, nt),
        in_specs=[blk, blk, blk, blk,
                  pl.BlockSpec((L, dh), lambda b, hh, i: (rb0 + b * nt + i, nh + hh)),
                  pl.BlockSpec((L, 2 * nh), lambda b, hh, i: (b * nt + i, 0)),
                  pl.BlockSpec((2 * nh, L), lambda b, hh, i: (0, b * nt + i)),
                  par, par],
        out_specs=[blk,
                   pl.BlockSpec((None, None, dh, dh), lambda b, hh, i: (b, hh, 0, 0)),
                   pl.BlockSpec((None, None, 1, dh), lambda b, hh, i: (b, hh, 0, 0)),
                   pl.BlockSpec((None, None, 1, 1), lambda b, hh, i: (b, hh, 0, 0))],
        out_shape=[jax.ShapeDtypeStruct((up.shape[0], inner), F32),
                   jax.ShapeDtypeStruct((bsz, nh, dh, dh), F32),
                   jax.ShapeDtypeStruct((bsz, nh, 1, dh), F32),
                   jax.ShapeDtypeStruct((bsz, nh, 1, 1), F32)],
        compiler_params=_params(3, 48),
        name="ml_seq",
    )(q, k, v, xc, up, gates, gates_t, gain3, skip3)


def _ml_step_body(bb, q_ref, k_ref, v_ref, xc_ref, z_ref, qc_ref, kc_ref, ig_ref, lf_ref, m_ref,
                  gain_ref, skip_ref, c_ref, n_ref, *rest):
    o_ref, co_ref, no_ref, mo_ref = rest[-4:]
    dh = q_ref.shape[-1]
    scale = dh ** -0.5
    ig_all = ig_ref[...]
    lf_all = _log_sigmoid(lf_ref[...])
    m_all = m_ref[...]
    g_all = lf_all + m_all
    mt_all = jnp.maximum(g_all, ig_all)
    w_all = jnp.exp(ig_all - mt_all)
    gi_all = jnp.exp(g_all - mt_all)
    mo_ref[...] = mt_all
    for n in range(bb):
        c_old = c_ref[n]
        n_old = n_ref[n]
        q_r = q_ref[n]
        k_r = k_ref[n] * scale
        v_r = v_ref[n]
        q_c = qc_ref[:, n:n + 1]
        k_c = kc_ref[:, n:n + 1] * scale
        mt = mt_all[:, n:n + 1]
        w = w_all[:, n:n + 1]
        gi = gi_all[:, n:n + 1]
        qk = jnp.sum(q_r * k_r, axis=-1, keepdims=True) * w
        num = qk * v_r + gi * jnp.sum(q_c * c_old, axis=0, keepdims=True)
        den = qk + gi * jnp.sum(q_r * n_old, axis=-1, keepdims=True)
        h = num / jnp.maximum(jnp.abs(den), jnp.exp(-mt))
        co_ref[n] = gi * c_old + (w * k_c) * v_r
        no_ref[n] = gi * n_old + w * k_r
        o_ref[n:n + 1, :] = _ml_out(h, gain_ref[...], skip_ref[...], xc_ref[n], z_ref[n])


def _ml_step(q3, k3, v3, xc3, up3, q_t, k_t, ig_t, lf_t, m_t, j, gain3, skip3, c_all, n_all, c_prev,
             y_all, row0, bb=8):
    bsz, _, inner = q3.shape
    nh = ML_HEADS
    dh = inner // nh
    row = pl.BlockSpec((bb, 1, dh), lambda hh, i: (i, 0, hh))
    col = pl.BlockSpec((None, None, dh, bb), lambda hh, i: (hh, i, 0, 0))
    sc = pl.BlockSpec((None, None, 1, bb), lambda hh, i: (hh, i, 0, 0))
    par = pl.BlockSpec((None, 1, dh), lambda hh, i: (j, 0, hh))
    nst = pl.BlockSpec((bb, None, 1, dh), lambda hh, i: (i, hh, 0, 0))
    cst = pl.BlockSpec((None, bb, None, dh, dh), lambda hh, i: (j, i, hh, 0, 0))
    nst_in = pl.BlockSpec((None, bb, None, 1, dh), lambda hh, i: (j, i, hh, 0, 0))
    in_specs = [row, row, row, row,
                pl.BlockSpec((bb, 1, dh), lambda hh, i: (i, 0, nh + hh)),
                col, col, sc, sc, sc, par, par, cst, nst_in]
    args = [q3, k3, v3, xc3, up3, q_t, k_t, ig_t, lf_t, m_t, gain3, skip3, c_all, n_all]
    aliases = _shared_out(y_all, in_specs, args, 0)
    aliases.update(_shared_out(c_prev, in_specs, args, 1))
    rb0 = row0 // bb
    return pl.pallas_call(
        functools.partial(_ml_step_body, bb),
        grid=(nh, bsz // bb),
        in_specs=in_specs,
        out_specs=[pl.BlockSpec((bb, dh), lambda hh, i: (rb0 + i, hh)), cst, nst, sc],
        out_shape=[jax.ShapeDtypeStruct(y_all.shape, F32),
                   jax.ShapeDtypeStruct(c_all.shape, F32),
                   jax.ShapeDtypeStruct(n_all.shape[1:], F32),
                   jax.ShapeDtypeStruct(m_t.shape, F32)],
        input_output_aliases=aliases,
        compiler_params=_params(2, 56),
        name="ml_step",
    )(*args)


def _block_diag(w, size):
    n, c, _ = w.shape
    per = size // c
    w4 = w.reshape(n // per, per, c, c)
    eye = jnp.eye(per, dtype=w.dtype)
    out = w4[:, :, :, None, :] * eye[None, :, None, :, None]
    return out.reshape(n // per, size, size).astype(BF16)


def _to_cols(x, nh, bb):
    bsz, w = x.shape
    d = w // nh
    return x.reshape(bsz // bb, bb, nh, d).transpose(2, 0, 3, 1)


def _s5_layout(bbt_r, bbt_i, c_re, c_im, n_chunks):
    ch, n_state = bbt_r.shape
    g = c_re.shape[0]
    p = n_state // g
    per = g // n_chunks
    eye = jnp.eye(per, dtype=F32)

    def bmat(bt):
        b4 = bt.reshape(ch, n_chunks, per, p).transpose(1, 0, 2, 3)
        out = eye[None, :, None, :, None] * b4[:, None, :, :, :]
        return out.reshape(n_chunks, per * ch, per * p).astype(BF16)

    def cmat(c):
        c4 = c.reshape(n_chunks, per, ch, p).transpose(0, 1, 3, 2)
        out = c4[:, :, :, None, :] * eye[None, :, None, :, None]
        return out.reshape(n_chunks, per * p, per * ch).astype(BF16)

    return bmat(bbt_r), bmat(bbt_i), cmat(c_re), cmat(c_im)


TM_FFN = 1040
TF_FFN = 256
TM_PROJ = 2080
TN_PROJ = 512
TM_OUT = 1040
TK_OUT = 512
TT_S5 = 256
TT_HGRN = 1024
TT_MLPRE = 512
LC_MLPRE = 512
L_ML = 256
S5_CHUNKS = 8


def kernel(x_prompt, x_sample, state_s5_re, state_s5_im, state_hgrn, state_conv, state_mlstm_c, state_mlstm_n, state_mlstm_m, norm_gain, ffn_w_gate, ffn_w_up, ffn_w_down, ab_w_in, ab_w_out, s5_lambda_re, s5_lambda_im, s5_log_step, s5_b_re, s5_b_im, s5_c_re, s5_c_im, s5_d, s5_glu_w, s5_glu_b, hgrn_lb_logits, hgrn_norm_gain, ml_w_up, ml_conv_w, ml_conv_b, ml_w_q, ml_w_k, ml_w_v, ml_w_gates, ml_b_gates, ml_norm_gain, ml_skip, ml_w_down):
    bp, t, d = x_prompt.shape
    bs = x_sample.shape[0]
    depth = norm_gain.shape[0]
    mp = bp * t
    n_ab, g5, p5 = s5_lambda_re.shape
    n_state = g5 * p5
    s5_w = g5 * S5_GROUP
    hg_w = HG_HEADS * LANE
    inner = ml_conv_w.shape[-1]
    n_c = ml_conv_w.shape[0]
    dh = inner // ML_HEADS

    gains = norm_gain.reshape(depth, norm_gain.shape[1], 1, d)
    h = jnp.concatenate([x_prompt.reshape(mp, d), x_sample.reshape(bs, d)], axis=0)

    out = dict(p_s5r=[], p_s5i=[], p_hg=[], p_conv=[], p_c=[], p_n=[], p_m=[],
               s_s5r=[], s_s5i=[], s_conv=[], s_n=[], s_m=[])
    s_hg = s_c = None

    for layer in range(depth):
        j = layer // 2
        h = _ffn(h, gains, ffn_w_gate, ffn_w_up, ffn_w_down, layer, 0, 0, 1, TM_FFN, TF_FFN)
        if layer % 2 == 0:
            proj = _norm_matmul(h, gains, layer, 2, ab_w_in, j, TM_PROJ, TN_PROJ)
            prm = _s5_params(s5_lambda_re[j].reshape(1, n_state), s5_lambda_im[j].reshape(1, n_state),
                             jnp.repeat(s5_log_step[j], p5).reshape(1, n_state),
                             s5_b_re[j].transpose(2, 0, 1).reshape(S5_GROUP, n_state),
                             s5_b_im[j].transpose(2, 0, 1).reshape(S5_GROUP, n_state))
            ap2r, ap2i, tabr, tabi, bbt_r, bbt_i = prm
            bbr, bbi, ccr, cci = _s5_layout(bbt_r, bbt_i, s5_c_re[j], s5_c_im[j], S5_CHUNKS)
            s5p = (ap2r, ap2i, tabr, tabi, bbr, bbi, ccr, cci)
            d3 = s5_d.reshape(n_ab, 1, s5_w)
            gb3 = s5_glu_b.reshape(n_ab, 1, s5_w)
            ya, sr_p, si_p = _s5_seq(proj, 0, bp, t, TT_S5, j, s5p, d3, s5_glu_w, gb3)
            ya, sr_s, si_s = _s5_step(proj, mp, bs, j, s5p, d3, s5_glu_w, gb3,
                                      state_s5_re[j].reshape(bs, n_state), state_s5_im[j].reshape(bs, n_state), ya)
            out["p_s5r"].append(sr_p.reshape(bp, g5, p5))
            out["p_s5i"].append(si_p.reshape(bp, g5, p5))
            out["s_s5r"].append(sr_s.reshape(bs, g5, p5))
            out["s_s5i"].append(si_s.reshape(bs, g5, p5))
            hgain3 = hgrn_norm_gain.reshape(n_ab, 1, hg_w)
            yb, hg_p = _hgrn_seq(proj, 0, bp, t, TT_HGRN, j, s5_w, hgrn_lb_logits, hgain3)
            proj_s = proj[mp:]
            bb_h = 8
            yb, s_hg = _hgrn_step(
                proj_s.reshape(bs, 1, proj.shape[1]),
                (_to_cols(proj_s[:, s5_w:s5_w + hg_w], HG_HEADS, bb_h),
                 _to_cols(proj_s[:, s5_w + hg_w:s5_w + 2 * hg_w], HG_HEADS, bb_h)),
                j, s5_w, hgrn_lb_logits, hgrn_lb_logits.T, hgain3, state_hgrn, s_hg, yb, mp, bb_h)
            out["p_hg"].append(hg_p)
            h = _matmul_postnorm([ya, yb], ab_w_out, j, h, gains, layer, 3, TM_OUT, TK_OUT)
        else:
            up = _norm_matmul(h, gains, layer, 2, ml_w_up, j, TM_PROJ, TN_PROJ)
            bd = (_block_diag(ml_w_q[j], MXU_DIM)[None], _block_diag(ml_w_k[j], MXU_DIM)[None],
                  _block_diag(ml_w_v[j], MXU_DIM)[None])
            wg4 = ml_w_gates[j].reshape(1, 3, inner, 2 * ML_HEADS)
            bg3 = ml_b_gates[j].reshape(1, 1, 2 * ML_HEADS)
            cw = ml_conv_w[j][None]
            cb3 = ml_conv_b[j].reshape(1, 1, inner)
            gain3 = ml_norm_gain.reshape(n_c, 1, inner)
            skip3 = ml_skip.reshape(n_c, 1, inner)
            q_p, k_p, v_p, xc_p, g_p = _ml_pre(up, 0, bp, t, TT_MLPRE, LC_MLPRE, 0, cw, cb3, bd, wg4, bg3)
            hf, c_p, n_p, m_p = _ml_seq(q_p, k_p, v_p, xc_p, up, 0, g_p, g_p.T, bp, t, L_ML, j, gain3, skip3)
            out["p_conv"].append(up[:mp, :inner].reshape(bp, t, inner)[:, t - (ML_CONV - 1):])
            out["p_c"].append(c_p)
            out["p_n"].append(n_p.reshape(bp, ML_HEADS, dh))
            out["p_m"].append(m_p.reshape(bp, ML_HEADS))
            taps = state_conv[j].transpose(1, 0, 2)
            q_s, k_s, v_s, xc_s, g_s = _ml_pre(up, mp, bs, 1, None, LC_MLPRE, 0, cw, cb3, bd, wg4, bg3, taps=taps)
            bb_m = 8
            sc = lambda a: a.T.reshape(ML_HEADS, bs // bb_m, 1, bb_m)
            r3 = lambda a: a.reshape(bs, 1, a.shape[-1])
            hf, s_c, n_s, m_s = _ml_step(
                r3(q_s), r3(k_s), r3(v_s), r3(xc_s), r3(up[mp:]),
                _to_cols(q_s, ML_HEADS, bb_m), _to_cols(k_s, ML_HEADS, bb_m),
                sc(g_s[:, :ML_HEADS]), sc(g_s[:, ML_HEADS:]), sc(state_mlstm_m[j]),
                j, gain3, skip3, state_mlstm_c, state_mlstm_n.reshape(n_c, bs, ML_HEADS, 1, dh), s_c,
                hf, mp, bb_m)
            out["s_conv"].append(jnp.concatenate([state_conv[j][:, 1:], up[mp:, None, :inner]], axis=1))
            out["s_n"].append(n_s.reshape(bs, ML_HEADS, dh))
            out["s_m"].append(m_s.reshape(ML_HEADS, bs).T)
            h = _matmul_postnorm([hf], ml_w_down, j, h, gains, layer, 3, TM_OUT, TK_OUT)
        h = _ffn(h, gains, ffn_w_gate, ffn_w_up, ffn_w_down, layer, 1, 4, 5, TM_FFN, TF_FFN)

    st = lambda name: jnp.stack(out[name])
    return (h[:mp].reshape(bp, t, d), h[mp:].reshape(bs, 1, d),
            st("p_s5r"), st("p_s5i"), st("p_hg"), st("p_conv"), st("p_c"), st("p_n"), st("p_m"),
            st("s_s5r"), st("s_s5i"), s_hg, st("s_conv"), s_c, st("s_n"), st("s_m"))
```

```python
import functools
import math

import jax
import jax.numpy as jnp
from jax import lax
from jax.experimental import pallas as pl
from jax.experimental.pallas import tpu as pltpu

F32 = jnp.float32
BF16 = jnp.bfloat16
EPS = 1e-6
MIB = 1024 * 1024

S5_GROUP = 16
S5_STATE = 64
HG_HEADS = 8
ML_HEADS = 8
ML_CONV = 4
ML_BLOCK = 4

LOG2_E = 1.4426950408889634

LANE = 128
MXU_DIM = 256


def _dot(a, b):
    return jnp.dot(a.astype(BF16), b.astype(BF16), preferred_element_type=F32)


def _dot_nt(a, b):
    return lax.dot_general(a.astype(BF16), b.astype(BF16), (((1,), (1,)), ((), ())),
                           preferred_element_type=F32)


def _dot_tn(a, b):
    return lax.dot_general(a.astype(BF16), b.astype(BF16), (((0,), (0,)), ((), ())),
                           preferred_element_type=F32)


def _sigmoid(x):
    return 1.0 / (1.0 + jnp.exp(-x))


def _silu(x):
    return x * _sigmoid(x)


def _log_sigmoid(x):
    return jnp.minimum(x, 0.0) - jnp.log1p(jnp.exp(-jnp.abs(x)))


def _logaddexp(a, b):
    return jnp.maximum(a, b) + jnp.log1p(jnp.exp(-jnp.abs(a - b)))


def _gelu_tanh(x):
    return x * (0.5 * (1.0 + jnp.tanh(math.sqrt(2.0 / math.pi) * (x + 0.044715 * (x * x * x)))))


def _rms(x, g):
    return x * lax.rsqrt(jnp.mean(x * x, axis=-1, keepdims=True) + EPS) * g


def _shift_rows(x, s):
    n = x.shape[0]
    if s % 8 == 0:
        return jnp.concatenate([jnp.zeros((s,) + x.shape[1:], x.dtype), x[:n - s]], axis=0)
    row = lax.broadcasted_iota(jnp.int32, x.shape, 0)
    return jnp.where(row >= s, pltpu.roll(x, s, 0), 0.0)


def _shared_out(prev, in_specs, args, out_index):
    if prev is None:
        return {}
    in_specs.append(pl.BlockSpec(memory_space=pl.ANY))
    args.append(prev)
    return {len(args) - 1: out_index}


def _params(n_axes, vmem_mib):
    return pltpu.CompilerParams(dimension_semantics=("arbitrary",) * n_axes,
                                vmem_limit_bytes=vmem_mib * MIB)


def _ffn_body(has_tail, *refs):
    if has_tail:
        (x_ref, gpre_ref, gpost_ref, wg_ref, wu_ref, wd_ref, wgt_ref, wut_ref, wdt_ref,
         o_ref, xn_ref, h_ref) = refs
    else:
        (x_ref, gpre_ref, gpost_ref, wg_ref, wu_ref, wd_ref, o_ref, xn_ref, h_ref) = refs
    j = pl.program_id(1)
    nj = pl.num_programs(1) - 1

    @pl.when(j == 0)
    def _():
        xn_ref[...] = _rms(x_ref[...], gpre_ref[...]).astype(BF16)
        o_ref[...] = jnp.zeros_like(o_ref)
        h_ref[...] = jnp.zeros_like(h_ref)

    def hidden(wg, wu):
        xn = xn_ref[...]
        return (_silu(_dot(xn, wg)) * _dot(xn, wu)).astype(BF16)

    @pl.when(j < nj)
    def _():
        o_ref[...] += _dot(h_ref[...], wd_ref[...])
        h_ref[...] = hidden(wg_ref[...], wu_ref[...])

    @pl.when(j == nj)
    def _():
        acc = o_ref[...] + _dot(h_ref[...], wd_ref[...])
        if has_tail:
            wt = jnp.concatenate([wgt_ref[...].astype(BF16), wut_ref[...].astype(BF16)], axis=-1)
            gu = _dot(xn_ref[...], wt)
            ht = (_silu(gu[:, :LANE]) * gu[:, LANE:]).astype(BF16)
            acc = acc + _dot(ht, wdt_ref[...])
        o_ref[...] = x_ref[...] + 0.5 * _rms(acc, gpost_ref[...])


def _ffn(h, gains, wg, wu, wd, layer, idx, k_pre, k_post, tm, tf):
    m, d = h.shape
    dff = wg.shape[-1]
    nj = dff // tf
    tail = dff - nj * tf
    assert m % tm == 0 and tail in (0, LANE) and nj >= 1
    tail_blk = dff // LANE - 1
    one = pl.Buffered(1)
    in_specs = [
        pl.BlockSpec((tm, d), lambda i, j: (i, 0), pipeline_mode=one),
        pl.BlockSpec((None, None, 1, d), lambda i, j: (layer, k_pre, 0, 0)),
        pl.BlockSpec((None, None, 1, d), lambda i, j: (layer, k_post, 0, 0)),
        pl.BlockSpec((None, None, d, tf), lambda i, j: (layer, idx, 0, jnp.minimum(j, nj - 1))),
        pl.BlockSpec((None, None, d, tf), lambda i, j: (layer, idx, 0, jnp.minimum(j, nj - 1))),
        pl.BlockSpec((None, None, tf, d), lambda i, j: (layer, idx, jnp.maximum(j - 1, 0), 0)),
    ]
    args = [h, gains, gains, wg, wu, wd]
    if tail:
        in_specs += [
            pl.BlockSpec((None, None, d, LANE), lambda i, j: (layer, idx, 0, tail_blk), pipeline_mode=one),
            pl.BlockSpec((None, None, d, LANE), lambda i, j: (layer, idx, 0, tail_blk), pipeline_mode=one),
            pl.BlockSpec((None, None, LANE, d), lambda i, j: (layer, idx, tail_blk, 0), pipeline_mode=one),
        ]
        args += [wg, wu, wd]
    return pl.pallas_call(
        functools.partial(_ffn_body, bool(tail)),
        grid=(m // tm, nj + 1),
        in_specs=in_specs,
        out_specs=pl.BlockSpec((tm, d), lambda i, j: (i, 0)),
        out_shape=jax.ShapeDtypeStruct((m, d), F32),
        scratch_shapes=[pltpu.VMEM((tm, d), BF16), pltpu.VMEM((tm, tf), BF16)],
        compiler_params=_params(2, 58),
        name="ffn",
    )(*args)


def _nmm_body(x_ref, g_ref, w_ref, o_ref, xn_ref):
    @pl.when(pl.program_id(1) == 0)
    def _():
        xn_ref[...] = _rms(x_ref[...], g_ref[...]).astype(BF16)

    o_ref[...] = _dot(xn_ref[...], w_ref[...])


def _norm_matmul(h, gains, layer, k_gain, w, j_w, tm, tn):
    m, d = h.shape
    n = w.shape[-1]
    assert m % tm == 0 and n % tn == 0
    return pl.pallas_call(
        _nmm_body,
        grid=(m // tm, n // tn),
        in_specs=[
            pl.BlockSpec((tm, d), lambda i, j: (i, 0), pipeline_mode=pl.Buffered(1)),
            pl.BlockSpec((None, None, 1, d), lambda i, j: (layer, k_gain, 0, 0)),
            pl.BlockSpec((None, d, tn), lambda i, j: (j_w, 0, j)),
        ],
        out_specs=pl.BlockSpec((tm, tn), lambda i, j: (i, j)),
        out_shape=jax.ShapeDtypeStruct((m, n), F32),
        scratch_shapes=[pltpu.VMEM((tm, d), BF16)],
        compiler_params=_params(2, 48),
        name="norm_matmul",
    )(h, gains, w)


def _mmpn_body(steps, *refs):
    n_in = len(steps)
    y_refs = refs[:n_in]
    w_ref, h_ref, g_ref, o_ref = refs[n_in:]
    k = pl.program_id(1)
    last = pl.num_programs(1) - 1

    @pl.when(k == 0)
    def _():
        o_ref[...] = jnp.zeros_like(o_ref)

    s0 = 0
    for y_ref, ns in zip(y_refs, steps):
        @pl.when((k >= s0) & (k < s0 + ns))
        def _(y_ref=y_ref):
            o_ref[...] += _dot(y_ref[...], w_ref[...])
        s0 += ns

    @pl.when(k == last)
    def _():
        o_ref[...] = h_ref[...] + _rms(o_ref[...], g_ref[...])


def _matmul_postnorm(ys, w, j_w, h, gains, layer, k_gain, tm, tk):
    m, d = h.shape
    steps = tuple(y.shape[1] // tk for y in ys)
    assert all(y.shape[1] % tk == 0 for y in ys) and sum(steps) * tk == w.shape[1] and m % tm == 0
    in_specs = []
    s0 = 0
    for ns in steps:
        in_specs.append(pl.BlockSpec(
            (tm, tk), lambda i, k, s0=s0, ns=ns: (i, jnp.clip(k - s0, 0, ns - 1))))
        s0 += ns
    in_specs += [
        pl.BlockSpec((None, tk, d), lambda i, k: (j_w, k, 0)),
        pl.BlockSpec((tm, d), lambda i, k: (i, 0), pipeline_mode=pl.Buffered(1)),
        pl.BlockSpec((None, None, 1, d), lambda i, k: (layer, k_gain, 0, 0)),
    ]
    return pl.pallas_call(
        functools.partial(_mmpn_body, steps),
        grid=(m // tm, sum(steps)),
        in_specs=in_specs,
        out_specs=pl.BlockSpec((tm, d), lambda i, k: (i, 0)),
        out_shape=jax.ShapeDtypeStruct((m, d), F32),
        compiler_params=_params(2, 56),
        name="matmul_postnorm",
    )(*ys, w, h, gains)


def _s5_param_body(lre_ref, lim_ref, ls_ref, br_ref, bi_ref,
                   ap2r_ref, ap2i_ref, tabr_ref, tabi_ref, bbr_ref, bbi_ref):
    lr = jnp.minimum(lre_ref[...], -1e-4)
    li = lim_ref[...]
    step = jnp.exp(ls_ref[...])
    mag = jnp.exp(lr * step)
    ang = li * step
    ar = mag * jnp.cos(ang)
    ai = mag * jnp.sin(ang)
    den = lr * lr + li * li
    nr = ar - 1.0
    gr = (nr * lr + ai * li) / den
    gi = (ai * lr - nr * li) / den
    br = br_ref[...]
    bi = bi_ref[...]
    bbr_ref[...] = gr * br - gi * bi
    bbi_ref[...] = gr * bi + gi * br
    row = lax.broadcasted_iota(jnp.int32, (8, ar.shape[1]), 0)
    pr, pi = ar, ai
    tabr_ref[0:1, :] = ar
    tabi_ref[0:1, :] = ai
    for k in range(ap2r_ref.shape[0]):
        ap2r_ref[k:k + 1, :] = pr
        ap2i_ref[k:k + 1, :] = pi
        n = 2 ** k
        if n < 8:
            tr = tabr_ref[0:n, :]
            ti = tabi_ref[0:n, :]
            tabr_ref[n:2 * n, :] = tr * pr - ti * pi
            tabi_ref[n:2 * n, :] = tr * pi + ti * pr
            tabr_ref[8 * (k + 1):8 * (k + 2), :] = jnp.where(row >= n, pr, 0.0)
            tabi_ref[8 * (k + 1):8 * (k + 2), :] = jnp.where(row >= n, pi, 0.0)
        pr, pi = pr * pr - pi * pi, pr * pi + pi * pr


S5_TAB_ROWS = 32


def _s5_params(lam_re, lam_im, log_step, b_re, b_im):
    n = lam_re.shape[-1]
    c = b_re.shape[0]
    tt = S5_TAB_ROWS
    full = lambda shape: pl.BlockSpec(shape, lambda: (0,) * len(shape))
    return pl.pallas_call(
        _s5_param_body,
        in_specs=[full((1, n))] * 3 + [full((c, n))] * 2,
        out_specs=[full((8, n))] * 2 + [full((tt, n))] * 2 + [full((c, n))] * 2,
        out_shape=[jax.ShapeDtypeStruct((8, n), F32)] * 2 + [jax.ShapeDtypeStruct((tt, n), F32)] * 2
        + [jax.ShapeDtypeStruct((c, n), F32)] * 2,
        compiler_params=pltpu.CompilerParams(vmem_limit_bytes=32 * MIB),
        name="s5_params",
    )(lam_re, lam_im, log_step, b_re, b_im)


def _s5_body(scan, n_chunks, *refs):
    if scan:
        (u_ref, ap2r_ref, ap2i_ref, tabr_ref, tabi_ref, bbr_ref, bbi_ref, ccr_ref, cci_ref, d_ref,
         gw_ref, gb_ref, y_ref, sr_ref, si_ref, cr_s, ci_s) = refs
    else:
        (u_ref, ap2r_ref, ap2i_ref, bbr_ref, bbi_ref, ccr_ref, cci_ref, d_ref,
         gw_ref, gb_ref, h0r_ref, h0i_ref) = refs[:12]
        y_ref, sr_ref, si_ref = refs[-3:]
    tt = u_ref.shape[0]
    cw = u_ref.shape[1] // n_chunks
    sw = bbr_ref.shape[2]

    if scan:
        @pl.when(pl.program_id(1) == 0)
        def _():
            cr_s[...] = jnp.zeros_like(cr_s)
            ci_s[...] = jnp.zeros_like(ci_s)

    u = u_ref[...]
    ys = []
    for c in range(n_chunks):
        uc = u[:, c * cw:(c + 1) * cw]
        lanes = slice(c * sw, (c + 1) * sw)
        xr = _dot(uc, bbr_ref[c])
        xi = _dot(uc, bbi_ref[c])
        if scan:
            xr = xr.reshape(tt // 8, 8, sw)
            xi = xi.reshape(tt // 8, 8, sw)
            for k in range(3):
                pr = tabr_ref[8 * (k + 1):8 * (k + 2), lanes]
                pi = tabi_ref[8 * (k + 1):8 * (k + 2), lanes]
                sr = pltpu.roll(xr, 2 ** k, 1)
                si = pltpu.roll(xi, 2 ** k, 1)
                xr, xi = xr + (pr * sr - pi * si), xi + (pr * si + pi * sr)
            tr = tabr_ref[0:8, lanes]
            ti = tabi_ref[0:8, lanes]
            hr = cr_s[:, lanes]
            hi = ci_s[:, lanes]
            out_r, out_i = [], []
            for g in range(tt // 8):
                gr = xr[g] + (tr * hr - ti * hi)
                gi = xi[g] + (tr * hi + ti * hr)
                out_r.append(gr)
                out_i.append(gi)
                hr, hi = gr[7:8], gi[7:8]
            xr = jnp.concatenate(out_r, axis=0)
            xi = jnp.concatenate(out_i, axis=0)
            cr_s[:, lanes] = hr
            ci_s[:, lanes] = hi
        else:
            tr = ap2r_ref[0:1, lanes]
            ti = ap2i_ref[0:1, lanes]
            hr = h0r_ref[:, lanes]
            hi = h0i_ref[:, lanes]
            xr, xi = xr + (tr * hr - ti * hi), xi + (tr * hi + ti * hr)
            sr_ref[:, lanes] = xr
            si_ref[:, lanes] = xi
        ys.append(_dot(xr, ccr_ref[c]) - _dot(xi, cci_ref[c]) + d_ref[:, c * cw:(c + 1) * cw] * uc)
    y = _gelu_tanh(jnp.concatenate(ys, axis=-1))
    y_ref[...] = y * _sigmoid(_dot(y, gw_ref[...]) + gb_ref[...])
    if scan:
        sr_ref[0] = cr_s[...]
        si_ref[0] = ci_s[...]


def _s5_specs(j, width, n_chunks, cw, sw, n_state, nidx):
    zero = (0,) * nidx
    return dict(
        ap2=pl.BlockSpec((8, n_state), lambda *a: (0, 0)),
        bb=pl.BlockSpec((n_chunks, cw, sw), lambda *a: (0, 0, 0)),
        cc=pl.BlockSpec((n_chunks, sw, cw), lambda *a: (0, 0, 0)),
        d=pl.BlockSpec((None, 1, width), lambda *a: (j, 0, 0)),
        gw=pl.BlockSpec((None, width, width), lambda *a: (j, 0, 0)),
        gb=pl.BlockSpec((None, 1, width), lambda *a: (j, 0, 0)),
    )


def _s5_seq(proj, row0, bsz, t, tt, j, prm, d3, glu_w, glu_b3):
    ap2r, ap2i, tabr, tabi, bbr, bbi, ccr, cci = prm
    n_chunks, cw, sw = bbr.shape
    width = n_chunks * cw
    n_state = n_chunks * sw
    nt = t // tt
    rb0 = row0 // tt
    sp = _s5_specs(j, width, n_chunks, cw, sw, n_state, 2)
    tab = pl.BlockSpec((S5_TAB_ROWS, n_state), lambda b, i: (0, 0))
    st = pl.BlockSpec((1, 1, n_state), lambda b, i: (b, 0, 0))
    y, sr, si = pl.pallas_call(
        functools.partial(_s5_body, True, n_chunks),
        grid=(bsz, nt),
        in_specs=[pl.BlockSpec((tt, width), lambda b, i: (rb0 + b * nt + i, 0)),
                  sp["ap2"], sp["ap2"], tab, tab, sp["bb"], sp["bb"], sp["cc"], sp["cc"],
                  sp["d"], sp["gw"], sp["gb"]],
        out_specs=[pl.BlockSpec((tt, width), lambda b, i: (b * nt + i, 0)), st, st],
        out_shape=[jax.ShapeDtypeStruct((proj.shape[0], width), F32),
                   jax.ShapeDtypeStruct((bsz, 1, n_state), F32),
                   jax.ShapeDtypeStruct((bsz, 1, n_state), F32)],
        scratch_shapes=[pltpu.VMEM((1, n_state), F32)] * 2,
        compiler_params=_params(2, 48),
        name="s5_seq",
    )(proj, ap2r, ap2i, tabr, tabi, bbr, bbi, ccr, cci, d3, glu_w, glu_b3)
    return y, sr[:, 0], si[:, 0]


def _s5_step(proj, row0, bsz, j, prm, d3, glu_w, glu_b3, h0r, h0i, y_all):
    ap2r, ap2i, _, _, bbr, bbi, ccr, cci = prm
    n_chunks, cw, sw = bbr.shape
    width = n_chunks * cw
    n_state = n_chunks * sw
    rb0 = row0 // bsz
    sp = _s5_specs(j, width, n_chunks, cw, sw, n_state, 1)
    st = pl.BlockSpec((bsz, n_state), lambda i: (0, 0))
    in_specs = [pl.BlockSpec((bsz, width), lambda i: (rb0, 0)),
                sp["ap2"], sp["ap2"], sp["bb"], sp["bb"], sp["cc"], sp["cc"],
                sp["d"], sp["gw"], sp["gb"], st, st]
    args = [proj, ap2r, ap2i, bbr, bbi, ccr, cci, d3, glu_w, glu_b3, h0r, h0i]
    aliases = _shared_out(y_all, in_specs, args, 0)
    return pl.pallas_call(
        functools.partial(_s5_body, False, n_chunks),
        grid=(1,),
        in_specs=in_specs,
        out_specs=[pl.BlockSpec((bsz, width), lambda i: (rb0, 0)), st, st],
        out_shape=[jax.ShapeDtypeStruct(y_all.shape, F32),
                   jax.ShapeDtypeStruct((bsz, n_state), F32),
                   jax.ShapeDtypeStruct((bsz, n_state), F32)],
        input_output_aliases=aliases,
        compiler_params=_params(1, 48),
        name="s5_step",
    )(*args)


def _hgrn_lower_bound(logits, j, axis):
    n = logits.shape[axis]
    mx = jnp.max(logits, axis=axis, keepdims=True)
    e = jnp.exp(logits - mx)
    sm = e / jnp.sum(e, axis=axis, keepdims=True)
    pick = (lambda i: sm[i:i + 1, :]) if axis == 0 else (lambda i: sm[:, i:i + 1])
    cum = pick(0)
    for i in range(1, j + 1):
        cum = cum + pick(i)
    assert j < n
    return cum - pick(0)


def _hgrn_gates(fl, lb):
    logf = _logaddexp(_log_sigmoid(fl), jnp.log(lb) + _log_sigmoid(-fl))
    k = (1.0 - lb) * _sigmoid(-fl)
    return logf, k


def _hgrn_seq_body(j, chunk, sub, q_ref, f_ref, v_ref, gt_ref, lg_ref, gain_ref,
                   y_ref, s_ref, st_s, q_s, k_s, v_s, b_s, o_s):
    tt, dk = q_ref.shape
    i = pl.program_id(2)

    @pl.when(i == 0)
    def _():
        st_s[...] = jnp.zeros_like(st_s)

    lb = _hgrn_lower_bound(lg_ref[...], j, 0)
    logf, k = _hgrn_gates(f_ref[...], lb)
    rin = lax.broadcasted_iota(jnp.int32, (tt, dk), 0) & (chunk - 1)
    b = logf
    s = 1
    while s < chunk:
        b = b + jnp.where(rin >= s, _shift_rows(b, s), 0.0)
        s *= 2
    q_s[...] = _silu(q_ref[...])
    k_s[...] = k
    v_s[...] = v_ref[...]
    b_s[...] = b * LOG2_E
    nsub = chunk // sub
    tsub = lax.broadcasted_iota(jnp.int32, (sub, dk), 0)

    def one_chunk(c, carry):
        r0 = pl.multiple_of(c * chunk, chunk)
        qc = q_s[pl.ds(r0, chunk), :]
        kc = k_s[pl.ds(r0, chunk), :]
        vc = v_s[pl.ds(r0, chunk), :]
        bc = b_s[pl.ds(r0, chunk), :]
        acc = [jnp.zeros((sub, dk), F32) for _ in range(nsub)]
        for s_i in range(chunk):
            ks = kc[s_i:s_i + 1]
            bs = bc[s_i:s_i + 1]
            vs = vc[s_i:s_i + 1]
            sb = s_i // sub
            for tb in range(sb, nsub):
                diff = bc[tb * sub:(tb + 1) * sub] - bs
                if tb == sb:
                    diff = jnp.where(tsub >= s_i - sb * sub, diff, -jnp.inf)
                p = qc[tb * sub:(tb + 1) * sub] * ks * jnp.exp2(diff)
                acc[tb] = acc[tb] + jnp.sum(p, axis=-1, keepdims=True) * vs
        st = st_s[...]
        o = jnp.concatenate(acc, axis=0) + _dot_nt(qc * jnp.exp2(bc), st)
        b_last = bc[chunk - 1:chunk]
        st_s[...] = st * jnp.exp2(b_last) + _dot_tn(vc, kc * jnp.exp2(b_last - bc))
        o_s[pl.ds(r0, chunk), :] = o
        return carry

    lax.fori_loop(0, tt // chunk, one_chunk, 0, unroll=8)
    o = o_s[...]
    o = o * lax.rsqrt(jnp.mean(o * o, axis=-1, keepdims=True) + EPS)
    y_ref[...] = o * gain_ref[...] * _silu(gt_ref[...])

    @pl.when(i == pl.num_programs(2) - 1)
    def _():
        s_ref[...] = st_s[...].T


def _hgrn_seq(proj, row0, bsz, t, tt, j, col0, logits, gain3, chunk=32, sub=8):
    h, dk = HG_HEADS, LANE
    nt = t // tt
    rb0 = row0 // tt
    cb = col0 // dk
    sec = lambda n: pl.BlockSpec((tt, dk), lambda b, hh, i: (rb0 + b * nt + i, cb + n * h + hh))
    y, s = pl.pallas_call(
        functools.partial(_hgrn_seq_body, j, chunk, sub),
        grid=(bsz, h, nt),
        in_specs=[sec(0), sec(1), sec(2), sec(3),
                  pl.BlockSpec((logits.shape[0], dk), lambda b, hh, i: (0, hh)),
                  pl.BlockSpec((None, 1, dk), lambda b, hh, i: (j, 0, hh))],
        out_specs=[pl.BlockSpec((tt, dk), lambda b, hh, i: (b * nt + i, hh)),
                   pl.BlockSpec((None, None, dk, dk), lambda b, hh, i: (b, hh, 0, 0))],
        out_shape=[jax.ShapeDtypeStruct((proj.shape[0], h * dk), F32),
                   jax.ShapeDtypeStruct((bsz, h, dk, dk), F32)],
        scratch_shapes=[pltpu.VMEM((dk, dk), F32)] + [pltpu.VMEM((tt, dk), F32)] * 5,
        compiler_params=_params(3, 32),
        name="hgrn_seq",
    )(proj, proj, proj, proj, logits, gain3)
    return y, s


def _hgrn_step_body(j, bb, q_ref, f_ref, v_ref, gt_ref, qc_ref, fc_ref, lg_ref, lgt_ref, gain_ref,
                    s_ref, *rest):
    y_ref, so_ref = rest[-2:]
    lb_r = _hgrn_lower_bound(lg_ref[...], j, 0)
    lb_c = _hgrn_lower_bound(lgt_ref[...], j, 1)
    qcol = _silu(qc_ref[...])
    logf_c, kcol = _hgrn_gates(fc_ref[...], lb_c)
    fcol = jnp.exp(logf_c)
    for n in range(bb):
        s = s_ref[n]
        q_r = _silu(q_ref[n])
        _, k_r = _hgrn_gates(f_ref[n], lb_r)
        v_r = v_ref[n]
        q_c = qcol[:, n:n + 1]
        f_c = fcol[:, n:n + 1]
        k_c = kcol[:, n:n + 1]
        score = jnp.sum(q_r * k_r, axis=-1, keepdims=True)
        o = score * v_r + jnp.sum((q_c * f_c) * s, axis=0, keepdims=True)
        so_ref[n] = f_c * s + k_c * v_r
        o = o * lax.rsqrt(jnp.mean(o * o, axis=-1, keepdims=True) + EPS)
        y_ref[n:n + 1, :] = o * gain_ref[...] * _silu(gt_ref[n])


def _hgrn_step(proj3, proj_t, j, col0, logits, logits_t, gain3, s_all, s_prev, y_all, row0, bb=8):
    bsz = proj3.shape[0]
    h, dk = HG_HEADS, LANE
    cb = col0 // dk
    qt, ft = proj_t
    row = lambda n: pl.BlockSpec((bb, 1, dk), lambda hh, i: (i, 0, cb + n * h + hh))
    col = pl.BlockSpec((None, None, dk, bb), lambda hh, i: (hh, i, 0, 0))
    st = pl.BlockSpec((None, bb, None, dk, dk), lambda hh, i: (j, i, hh, 0, 0))
    in_specs = [row(0), row(1), row(2), row(3), col, col,
                pl.BlockSpec((logits.shape[0], dk), lambda hh, i: (0, hh)),
                pl.BlockSpec((dk, logits.shape[0]), lambda hh, i: (hh, 0)),
                pl.BlockSpec((None, 1, dk), lambda hh, i: (j, 0, hh)),
                st]
    args = [proj3, proj3, proj3, proj3, qt, ft, logits, logits_t, gain3, s_all]
    aliases = _shared_out(y_all, in_specs, args, 0)
    aliases.update(_shared_out(s_prev, in_specs, args, 1))
    rb0 = row0 // bb
    return pl.pallas_call(
        functools.partial(_hgrn_step_body, j, bb),
        grid=(h, bsz // bb),
        in_specs=in_specs,
        out_specs=[pl.BlockSpec((bb, dk), lambda hh, i: (rb0 + i, hh)), st],
        out_shape=[jax.ShapeDtypeStruct(y_all.shape, F32),
                   jax.ShapeDtypeStruct(s_all.shape, F32)],
        input_output_aliases=aliases,
        compiler_params=_params(2, 32),
        name="hgrn_step",
    )(*args)


def _ml_conv_seq(xm, cw_ref, cb_ref, stage_s, halo):
    tt = xm.shape[0]
    stage_s[0:8, :] = halo
    stage_s[8:, :] = xm
    acc = cb_ref[...] + cw_ref[ML_CONV - 1:ML_CONV, :] * xm
    for d in range(1, ML_CONV):
        acc = acc + cw_ref[ML_CONV - 1 - d:ML_CONV - d, :] * stage_s[8 - d:8 - d + tt, :]
    return _silu(acc)


def _ml_qkv(xc, xm, bdq_ref, bdk_ref, bdv_ref):
    nblk = bdq_ref.shape[0]
    w = bdq_ref.shape[1]
    xcb = xc.astype(BF16)
    xmb = xm.astype(BF16)
    q = jnp.concatenate([_dot(xcb[:, s * w:(s + 1) * w], bdq_ref[s]) for s in range(nblk)], axis=-1)
    k = jnp.concatenate([_dot(xcb[:, s * w:(s + 1) * w], bdk_ref[s]) for s in range(nblk)], axis=-1)
    v = jnp.concatenate([_dot(xmb[:, s * w:(s + 1) * w], bdv_ref[s]) for s in range(nblk)], axis=-1)
    return q, k, v


def _ml_gates_acc(c, q, k, v, wg_ref, bg_ref, g_ref):
    g = _dot(q, wg_ref[0]) + _dot(k, wg_ref[1]) + _dot(v, wg_ref[2])

    @pl.when(c == 0)
    def _():
        g_ref[...] = g + bg_ref[...]

    @pl.when(c > 0)
    def _():
        g_ref[...] += g


def _ml_pre_seq_body(xm_ref, cw_ref, cb_ref, bdq_ref, bdk_ref, bdv_ref, wg_ref, bg_ref,
                     g_ref, stage_s, halo_s):
    i = pl.program_id(1)
    c = pl.program_id(2)
    tt = xm_ref.shape[0]

    @pl.when(i == 0)
    def _():
        halo_s[c] = jnp.zeros(halo_s.shape[1:], F32)

    xm = xm_ref[...]
    xc = _ml_conv_seq(xm, cw_ref, cb_ref, stage_s, halo_s[c])
    halo_s[c] = xm[tt - 8:tt]
    q, k, v = _ml_qkv(xc, xm, bdq_ref, bdk_ref, bdv_ref)
    _ml_gates_acc(c, q, k, v, wg_ref, bg_ref, g_ref)


def _ml_pre_step_body(xm_ref, taps_ref, cw_ref, cb_ref, bdq_ref, bdk_ref, bdv_ref, wg_ref, bg_ref,
                      q_ref, k_ref, v_ref, xc_ref, g_ref):
    c = pl.program_id(0)
    xm = xm_ref[...]
    acc = cb_ref[...] + cw_ref[ML_CONV - 1:ML_CONV, :] * xm
    for d in range(1, ML_CONV):
        acc = acc + cw_ref[ML_CONV - 1 - d:ML_CONV - d, :] * taps_ref[ML_CONV - 1 - d]
    xc = _silu(acc)
    q, k, v = _ml_qkv(xc, xm, bdq_ref, bdk_ref, bdv_ref)
    q_ref[...] = q
    k_ref[...] = k
    v_ref[...] = v
    xc_ref[...] = xc
    _ml_gates_acc(c, q, k, v, wg_ref, bg_ref, g_ref)


def _ml_pre(up, row0, bsz, t, tt, lc, j, conv_w, conv_b3, bd, wg4, bg3, taps=None):
    inner = conv_w.shape[-1]
    ng = wg4.shape[-1]
    nc = inner // lc
    nblk = lc // MXU_DIM
    bdq, bdk, bdv = bd
    seq = taps is None
    rows = bsz * t
    if seq:
        nt = t // tt
        rb0 = row0 // tt
        grid = (bsz, nt, nc)
        rmap = lambda b, i, c: b * nt + i
        cmap = lambda b, i, c: c
    else:
        tt = bsz
        rb0 = row0 // tt
        grid = (nc,)
        rmap = lambda c: 0
        cmap = lambda c: c
    in_specs = [pl.BlockSpec((tt, lc), lambda *a: (rb0 + rmap(*a), cmap(*a)))]
    args = [up]
    if not seq:
        in_specs.append(pl.BlockSpec((ML_CONV - 1, tt, lc), lambda *a: (0, 0, cmap(*a))))
        args.append(taps)
    in_specs += [
        pl.BlockSpec((None, ML_CONV, lc), lambda *a: (j, 0, cmap(*a))),
        pl.BlockSpec((None, 1, lc), lambda *a: (j, 0, cmap(*a))),
        pl.BlockSpec((None, nblk, MXU_DIM, MXU_DIM), lambda *a: (j, cmap(*a), 0, 0)),
        pl.BlockSpec((None, nblk, MXU_DIM, MXU_DIM), lambda *a: (j, cmap(*a), 0, 0)),
        pl.BlockSpec((None, nblk, MXU_DIM, MXU_DIM), lambda *a: (j, cmap(*a), 0, 0)),
        pl.BlockSpec((None, 3, lc, ng), lambda *a: (j, 0, cmap(*a), 0)),
        pl.BlockSpec((None, 1, ng), lambda *a: (j, 0, 0)),
    ]
    args += [conv_w, conv_b3, bdq, bdk, bdv, wg4, bg3]
    big = pl.BlockSpec((tt, lc), lambda *a: (rmap(*a), cmap(*a)))
    n_big = 0 if seq else 4
    return pl.pallas_call(
        _ml_pre_seq_body if seq else _ml_pre_step_body,
        grid=grid,
        in_specs=in_specs,
        out_specs=[big] * n_big + [pl.BlockSpec((tt, ng), lambda *a: (rmap(*a), 0))],
        out_shape=[jax.ShapeDtypeStruct((rows, inner), F32)] * n_big + [jax.ShapeDtypeStruct((rows, ng), F32)],
        scratch_shapes=([pltpu.VMEM((tt + 8, lc), F32), pltpu.VMEM((nc, 8, lc), F32)] if seq else []),
        compiler_params=_params(len(grid), 48),
        name="ml_pre_seq" if seq else "ml_pre_step",
    )(*args)


def _ml_out(h, gain, skip, xc, z):
    mu = jnp.mean(h, axis=-1, keepdims=True)
    hc = h - mu
    hn = hc * lax.rsqrt(jnp.mean(hc * hc, axis=-1, keepdims=True) + EPS)
    return (hn * gain + skip * xc) * _silu(z)


def _ml_seq_body(xm_ref, z_ref, g_ref, gt_ref, cw_ref, cb_ref, bdq_ref, bdk_ref, bdv_ref, gain_ref, skip_ref,
                 o_ref, c_ref, n_ref, m_ref, stage_s, halo_s):
    hh = pl.program_id(1)
    i = pl.program_id(2)
    L, dh = xm_ref.shape
    nh = g_ref.shape[1] // 2

    @pl.when(i == 0)
    def _():
        c_ref[...] = jnp.zeros_like(c_ref)
        n_ref[...] = jnp.zeros_like(n_ref)
        m_ref[...] = jnp.zeros_like(m_ref)
        halo_s[...] = jnp.zeros_like(halo_s)

    xm = xm_ref[...]
    xc = _ml_conv_seq(xm, cw_ref, cb_ref, stage_s, halo_s[...])
    halo_s[...] = xm[L - 8:L]
    q, k, v = _ml_qkv(xc, xm, bdq_ref, bdk_ref, bdv_ref)
    k = k * (dh ** -0.5)

    g = g_ref[...]
    gt = gt_ref[...]
    lane = lax.broadcasted_iota(jnp.int32, g.shape, 1)
    sub = lax.broadcasted_iota(jnp.int32, gt.shape, 0)
    ig_c = jnp.sum(jnp.where(lane == hh, g, 0.0), axis=1, keepdims=True)
    lf_c = _log_sigmoid(jnp.sum(jnp.where(lane == nh + hh, g, 0.0), axis=1, keepdims=True))
    ig_r = jnp.sum(jnp.where(sub == hh, gt, 0.0), axis=0, keepdims=True)
    lf_r = _log_sigmoid(jnp.sum(jnp.where(sub == nh + hh, gt, 0.0), axis=0, keepdims=True))
    tr = lax.broadcasted_iota(jnp.int32, (L, L), 0)
    tc = lax.broadcasted_iota(jnp.int32, (L, L), 1)
    causal = tr >= tc
    b_c = jnp.sum(jnp.where(causal, lf_r, 0.0), axis=1, keepdims=True)
    b_r = jnp.sum(jnp.where(tr <= tc, lf_c, 0.0), axis=0, keepdims=True)
    m = m_ref[...]
    dlog = jnp.where(causal, b_c - b_r + ig_r, -jnp.inf)
    gsum = b_c + m
    mt = jnp.maximum(gsum, jnp.max(dlog, axis=1, keepdims=True))
    w = jnp.exp(dlog - mt)
    gi = jnp.exp(gsum - mt)
    c_old = c_ref[...]
    n_old = n_ref[...]
    qk = _dot_nt(q, k) * w
    num = _dot(qk, v) + gi * _dot(q, c_old)
    den = jnp.sum(qk, axis=1, keepdims=True) + gi * jnp.sum(q * n_old, axis=1, keepdims=True)
    h = num / jnp.maximum(jnp.abs(den), jnp.exp(-mt))
    m_new = mt[L - 1:L]
    b_last = b_c[L - 1:L]
    decay = jnp.exp(b_last + m - m_new)
    ks = k * jnp.exp(b_last - b_c + ig_c - m_new)
    c_ref[...] = decay * c_old + _dot_tn(ks, v)
    n_ref[...] = decay * n_old + jnp.sum(ks, axis=0, keepdims=True)
    m_ref[...] = m_new
    o_ref[...] = _ml_out(h, gain_ref[...], skip_ref[...], xc, z_ref[...])


def _ml_seq(up, row0, gates, gates_t, bsz, t, L, j, gain3, skip3, conv_w, conv_b3, bd):
    inner = conv_w.shape[-1]
    nh = ML_HEADS
    dh = inner // nh
    nt = t // L
    rb0 = row0 // L
    nblk = dh // MXU_DIM
    blk = pl.BlockSpec((L, dh), lambda b, hh, i: (b * nt + i, hh))
    par = pl.BlockSpec((None, 1, dh), lambda b, hh, i: (j, 0, hh))
    tile = pl.BlockSpec((None, nblk, MXU_DIM, MXU_DIM), lambda b, hh, i: (0, hh, 0, 0))
    bdq, bdk, bdv = bd
    return pl.pallas_call(
        _ml_seq_body,
        grid=(bsz, nh, nt),
        in_specs=[pl.BlockSpec((L, dh), lambda b, hh, i: (rb0 + b * nt + i, hh)),
                  pl.BlockSpec((L, dh), lambda b, hh, i: (rb0 + b * nt + i, nh + hh)),
                  pl.BlockSpec((L, 2 * nh), lambda b, hh, i: (b * nt + i, 0)),
                  pl.BlockSpec((2 * nh, L), lambda b, hh, i: (0, b * nt + i)),
                  pl.BlockSpec((None, ML_CONV, dh), lambda b, hh, i: (0, 0, hh)),
                  pl.BlockSpec((None, 1, dh), lambda b, hh, i: (0, 0, hh)),
                  tile, tile, tile, par, par],
        out_specs=[blk,
                   pl.BlockSpec((None, None, dh, dh), lambda b, hh, i: (b, hh, 0, 0)),
                   pl.BlockSpec((None, None, 1, dh), lambda b, hh, i: (b, hh, 0, 0)),
                   pl.BlockSpec((None, None, 1, 1), lambda b, hh, i: (b, hh, 0, 0))],
        out_shape=[jax.ShapeDtypeStruct((up.shape[0], inner), F32),
                   jax.ShapeDtypeStruct((bsz, nh, dh, dh), F32),
                   jax.ShapeDtypeStruct((bsz, nh, 1, dh), F32),
                   jax.ShapeDtypeStruct((bsz, nh, 1, 1), F32)],
        scratch_shapes=[pltpu.VMEM((L + 8, dh), F32), pltpu.VMEM((8, dh), F32)],
        compiler_params=_params(3, 48),
        name="ml_seq",
    )(up, up, gates, gates_t, conv_w, conv_b3, bdq, bdk, bdv, gain3, skip3)


def _ml_step_body(bb, q_ref, k_ref, v_ref, xc_ref, z_ref, qc_ref, kc_ref, ig_ref, lf_ref, m_ref,
                  gain_ref, skip_ref, c_ref, n_ref, *rest):
    o_ref, co_ref, no_ref, mo_ref = rest[-4:]
    dh = q_ref.shape[-1]
    scale = dh ** -0.5
    ig_all = ig_ref[...]
    lf_all = _log_sigmoid(lf_ref[...])
    m_all = m_ref[...]
    g_all = lf_all + m_all
    mt_all = jnp.maximum(g_all, ig_all)
    w_all = jnp.exp(ig_all - mt_all)
    gi_all = jnp.exp(g_all - mt_all)
    mo_ref[...] = mt_all
    for n in range(bb):
        c_old = c_ref[n]
        n_old = n_ref[n]
        q_r = q_ref[n]
        k_r = k_ref[n] * scale
        v_r = v_ref[n]
        q_c = qc_ref[:, n:n + 1]
        k_c = kc_ref[:, n:n + 1] * scale
        mt = mt_all[:, n:n + 1]
        w = w_all[:, n:n + 1]
        gi = gi_all[:, n:n + 1]
        qk = jnp.sum(q_r * k_r, axis=-1, keepdims=True) * w
        num = qk * v_r + gi * jnp.sum(q_c * c_old, axis=0, keepdims=True)
        den = qk + gi * jnp.sum(q_r * n_old, axis=-1, keepdims=True)
        h = num / jnp.maximum(jnp.abs(den), jnp.exp(-mt))
        co_ref[n] = gi * c_old + (w * k_c) * v_r
        no_ref[n] = gi * n_old + w * k_r
        o_ref[n:n + 1, :] = _ml_out(h, gain_ref[...], skip_ref[...], xc_ref[n], z_ref[n])


def _ml_step(q3, k3, v3, xc3, up3, q_t, k_t, ig_t, lf_t, m_t, j, gain3, skip3, c_all, n_all, c_prev,
             y_all, row0, bb=8):
    bsz, _, inner = q3.shape
    nh = ML_HEADS
    dh = inner // nh
    row = pl.BlockSpec((bb, 1, dh), lambda hh, i: (i, 0, hh))
    col = pl.BlockSpec((None, None, dh, bb), lambda hh, i: (hh, i, 0, 0))
    sc = pl.BlockSpec((None, None, 1, bb), lambda hh, i: (hh, i, 0, 0))
    par = pl.BlockSpec((None, 1, dh), lambda hh, i: (j, 0, hh))
    nst = pl.BlockSpec((bb, None, 1, dh), lambda hh, i: (i, hh, 0, 0))
    cst = pl.BlockSpec((None, bb, None, dh, dh), lambda hh, i: (j, i, hh, 0, 0))
    nst_in = pl.BlockSpec((None, bb, None, 1, dh), lambda hh, i: (j, i, hh, 0, 0))
    in_specs = [row, row, row, row,
                pl.BlockSpec((bb, 1, dh), lambda hh, i: (i, 0, nh + hh)),
                col, col, sc, sc, sc, par, par, cst, nst_in]
    args = [q3, k3, v3, xc3, up3, q_t, k_t, ig_t, lf_t, m_t, gain3, skip3, c_all, n_all]
    aliases = _shared_out(y_all, in_specs, args, 0)
    aliases.update(_shared_out(c_prev, in_specs, args, 1))
    rb0 = row0 // bb
    return pl.pallas_call(
        functools.partial(_ml_step_body, bb),
        grid=(nh, bsz // bb),
        in_specs=in_specs,
        out_specs=[pl.BlockSpec((bb, dh), lambda hh, i: (rb0 + i, hh)), cst, nst, sc],
        out_shape=[jax.ShapeDtypeStruct(y_all.shape, F32),
                   jax.ShapeDtypeStruct(c_all.shape, F32),
                   jax.ShapeDtypeStruct(n_all.shape[1:], F32),
                   jax.ShapeDtypeStruct(m_t.shape, F32)],
        input_output_aliases=aliases,
        compiler_params=_params(2, 56),
        name="ml_step",
    )(*args)


def _block_diag(w, size):
    n, c, _ = w.shape
    per = size // c
    w4 = w.reshape(n // per, per, c, c)
    eye = jnp.eye(per, dtype=w.dtype)
    out = w4[:, :, :, None, :] * eye[None, :, None, :, None]
    return out.reshape(n // per, size, size).astype(BF16)


def _to_cols(x, nh, bb):
    bsz, w = x.shape
    d = w // nh
    return x.reshape(bsz // bb, bb, nh, d).transpose(2, 0, 3, 1)


def _s5_layout(bbt_r, bbt_i, c_re, c_im, n_chunks):
    ch, n_state = bbt_r.shape
    g = c_re.shape[0]
    p = n_state // g
    per = g // n_chunks
    eye = jnp.eye(per, dtype=F32)

    def bmat(bt):
        b4 = bt.reshape(ch, n_chunks, per, p).transpose(1, 0, 2, 3)
        out = eye[None, :, None, :, None] * b4[:, None, :, :, :]
        return out.reshape(n_chunks, per * ch, per * p).astype(BF16)

    def cmat(c):
        c4 = c.reshape(n_chunks, per, ch, p).transpose(0, 1, 3, 2)
        out = c4[:, :, :, None, :] * eye[None, :, None, :, None]
        return out.reshape(n_chunks, per * p, per * ch).astype(BF16)

    return bmat(bbt_r), bmat(bbt_i), cmat(c_re), cmat(c_im)


TM_FFN = 1040
TF_FFN = 256
TM_PROJ = 2080
TN_PROJ = 512
TM_OUT = 1040
TK_OUT = 512
TT_S5 = 256
TT_HGRN = 1024
TT_MLPRE = 512
LC_MLPRE = 512
L_ML = 256
S5_CHUNKS = 8


def kernel(x_prompt, x_sample, state_s5_re, state_s5_im, state_hgrn, state_conv, state_mlstm_c, state_mlstm_n, state_mlstm_m, norm_gain, ffn_w_gate, ffn_w_up, ffn_w_down, ab_w_in, ab_w_out, s5_lambda_re, s5_lambda_im, s5_log_step, s5_b_re, s5_b_im, s5_c_re, s5_c_im, s5_d, s5_glu_w, s5_glu_b, hgrn_lb_logits, hgrn_norm_gain, ml_w_up, ml_conv_w, ml_conv_b, ml_w_q, ml_w_k, ml_w_v, ml_w_gates, ml_b_gates, ml_norm_gain, ml_skip, ml_w_down):
    bp, t, d = x_prompt.shape
    bs = x_sample.shape[0]
    depth = norm_gain.shape[0]
    mp = bp * t
    n_ab, g5, p5 = s5_lambda_re.shape
    n_state = g5 * p5
    s5_w = g5 * S5_GROUP
    hg_w = HG_HEADS * LANE
    inner = ml_conv_w.shape[-1]
    n_c = ml_conv_w.shape[0]
    dh = inner // ML_HEADS

    gains = norm_gain.reshape(depth, norm_gain.shape[1], 1, d)
    h = jnp.concatenate([x_prompt.reshape(mp, d), x_sample.reshape(bs, d)], axis=0)

    out = dict(p_s5r=[], p_s5i=[], p_hg=[], p_conv=[], p_c=[], p_n=[], p_m=[],
               s_s5r=[], s_s5i=[], s_conv=[], s_n=[], s_m=[])
    s_hg = s_c = None

    for layer in range(depth):
        j = layer // 2
        h = _ffn(h, gains, ffn_w_gate, ffn_w_up, ffn_w_down, layer, 0, 0, 1, TM_FFN, TF_FFN)
        if layer % 2 == 0:
            proj = _norm_matmul(h, gains, layer, 2, ab_w_in, j, TM_PROJ, TN_PROJ)
            prm = _s5_params(s5_lambda_re[j].reshape(1, n_state), s5_lambda_im[j].reshape(1, n_state),
                             jnp.repeat(s5_log_step[j], p5).reshape(1, n_state),
                             s5_b_re[j].transpose(2, 0, 1).reshape(S5_GROUP, n_state),
                             s5_b_im[j].transpose(2, 0, 1).reshape(S5_GROUP, n_state))
            ap2r, ap2i, tabr, tabi, bbt_r, bbt_i = prm
            bbr, bbi, ccr, cci = _s5_layout(bbt_r, bbt_i, s5_c_re[j], s5_c_im[j], S5_CHUNKS)
            s5p = (ap2r, ap2i, tabr, tabi, bbr, bbi, ccr, cci)
            d3 = s5_d.reshape(n_ab, 1, s5_w)
            gb3 = s5_glu_b.reshape(n_ab, 1, s5_w)
            ya, sr_p, si_p = _s5_seq(proj, 0, bp, t, TT_S5, j, s5p, d3, s5_glu_w, gb3)
            ya, sr_s, si_s = _s5_step(proj, mp, bs, j, s5p, d3, s5_glu_w, gb3,
                                      state_s5_re[j].reshape(bs, n_state), state_s5_im[j].reshape(bs, n_state), ya)
            out["p_s5r"].append(sr_p.reshape(bp, g5, p5))
            out["p_s5i"].append(si_p.reshape(bp, g5, p5))
            out["s_s5r"].append(sr_s.reshape(bs, g5, p5))
            out["s_s5i"].append(si_s.reshape(bs, g5, p5))
            hgain3 = hgrn_norm_gain.reshape(n_ab, 1, hg_w)
            yb, hg_p = _hgrn_seq(proj, 0, bp, t, TT_HGRN, j, s5_w, hgrn_lb_logits, hgain3)
            proj_s = proj[mp:]
            bb_h = 8
            yb, s_hg = _hgrn_step(
                proj_s.reshape(bs, 1, proj.shape[1]),
                (_to_cols(proj_s[:, s5_w:s5_w + hg_w], HG_HEADS, bb_h),
                 _to_cols(proj_s[:, s5_w + hg_w:s5_w + 2 * hg_w], HG_HEADS, bb_h)),
                j, s5_w, hgrn_lb_logits, hgrn_lb_logits.T, hgain3, state_hgrn, s_hg, yb, mp, bb_h)
            out["p_hg"].append(hg_p)
            h = _matmul_postnorm([ya, yb], ab_w_out, j, h, gains, layer, 3, TM_OUT, TK_OUT)
        else:
            up = _norm_matmul(h, gains, layer, 2, ml_w_up, j, TM_PROJ, TN_PROJ)
            bd = (_block_diag(ml_w_q[j], MXU_DIM)[None], _block_diag(ml_w_k[j], MXU_DIM)[None],
                  _block_diag(ml_w_v[j], MXU_DIM)[None])
            wg4 = ml_w_gates[j].reshape(1, 3, inner, 2 * ML_HEADS)
            bg3 = ml_b_gates[j].reshape(1, 1, 2 * ML_HEADS)
            cw = ml_conv_w[j][None]
            cb3 = ml_conv_b[j].reshape(1, 1, inner)
            gain3 = ml_norm_gain.reshape(n_c, 1, inner)
            skip3 = ml_skip.reshape(n_c, 1, inner)
            (g_p,) = _ml_pre(up, 0, bp, t, TT_MLPRE, LC_MLPRE, 0, cw, cb3, bd, wg4, bg3)
            hf, c_p, n_p, m_p = _ml_seq(up, 0, g_p, g_p.T, bp, t, L_ML, j, gain3, skip3, cw, cb3, bd)
            out["p_conv"].append(jnp.stack([
                lax.slice(up, ((b + 1) * t - (ML_CONV - 1), 0), ((b + 1) * t, inner)) for b in range(bp)]))
            out["p_c"].append(c_p)
            out["p_n"].append(n_p.reshape(bp, ML_HEADS, dh))
            out["p_m"].append(m_p.reshape(bp, ML_HEADS))
            taps = state_conv[j].transpose(1, 0, 2)
            q_s, k_s, v_s, xc_s, g_s = _ml_pre(up, mp, bs, 1, None, LC_MLPRE, 0, cw, cb3, bd, wg4, bg3, taps=taps)
            bb_m = 8
            sc = lambda a: a.T.reshape(ML_HEADS, bs // bb_m, 1, bb_m)
            r3 = lambda a: a.reshape(bs, 1, a.shape[-1])
            hf, s_c, n_s, m_s = _ml_step(
                r3(q_s), r3(k_s), r3(v_s), r3(xc_s), r3(up[mp:]),
                _to_cols(q_s, ML_HEADS, bb_m), _to_cols(k_s, ML_HEADS, bb_m),
                sc(g_s[:, :ML_HEADS]), sc(g_s[:, ML_HEADS:]), sc(state_mlstm_m[j]),
                j, gain3, skip3, state_mlstm_c, state_mlstm_n.reshape(n_c, bs, ML_HEADS, 1, dh), s_c,
                hf, mp, bb_m)
            out["s_conv"].append(jnp.concatenate([state_conv[j][:, 1:], up[mp:, None, :inner]], axis=1))
            out["s_n"].append(n_s.reshape(bs, ML_HEADS, dh))
            out["s_m"].append(m_s.reshape(ML_HEADS, bs).T)
            h = _matmul_postnorm([hf], ml_w_down, j, h, gains, layer, 3, TM_OUT, TK_OUT)
        h = _ffn(h, gains, ffn_w_gate, ffn_w_up, ffn_w_down, layer, 1, 4, 5, TM_FFN, TF_FFN)

    st = lambda name: jnp.stack(out[name])
    return (h[:mp].reshape(bp, t, d), h[mp:].reshape(bs, 1, d),
            st("p_s5r"), st("p_s5i"), st("p_hg"), st("p_conv"), st("p_c"), st("p_n"), st("p_m"),
            st("s_s5r"), st("s_s5i"), s_hg, st("s_conv"), s_c, st("s_n"), st("s_m"))
```

```python
import functools
import math

import jax
import jax.numpy as jnp
from jax import lax
from jax.experimental import pallas as pl
from jax.experimental.pallas import tpu as pltpu

F32 = jnp.float32
BF16 = jnp.bfloat16
EPS = 1e-6
MIB = 1024 * 1024

S5_GROUP = 16
S5_STATE = 64
HG_HEADS = 8
ML_HEADS = 8
ML_CONV = 4
ML_BLOCK = 4

LOG2_E = 1.4426950408889634

LANE = 128
MXU_DIM = 256


def _dot(a, b):
    return jnp.dot(a.astype(BF16), b.astype(BF16), preferred_element_type=F32)


def _dot_nt(a, b):
    return lax.dot_general(a.astype(BF16), b.astype(BF16), (((1,), (1,)), ((), ())),
                           preferred_element_type=F32)


def _dot_tn(a, b):
    return lax.dot_general(a.astype(BF16), b.astype(BF16), (((0,), (0,)), ((), ())),
                           preferred_element_type=F32)


def _sigmoid(x):
    return 1.0 / (1.0 + jnp.exp(-x))


def _silu(x):
    return x * _sigmoid(x)


def _log_sigmoid(x):
    return jnp.minimum(x, 0.0) - jnp.log1p(jnp.exp(-jnp.abs(x)))


def _logaddexp(a, b):
    return jnp.maximum(a, b) + jnp.log1p(jnp.exp(-jnp.abs(a - b)))


def _gelu_tanh(x):
    return x * (0.5 * (1.0 + jnp.tanh(math.sqrt(2.0 / math.pi) * (x + 0.044715 * (x * x * x)))))


def _shift_rows(x, s):
    n = x.shape[0]
    if s % 8 == 0:
        return jnp.concatenate([jnp.zeros((s,) + x.shape[1:], x.dtype), x[:n - s]], axis=0)
    row = lax.broadcasted_iota(jnp.int32, x.shape, 0)
    return jnp.where(row >= s, pltpu.roll(x, s, 0), 0.0)


def _shared_out(prev, in_specs, args, out_index):
    if prev is None:
        return {}
    in_specs.append(pl.BlockSpec(memory_space=pl.ANY))
    args.append(prev)
    return {len(args) - 1: out_index}


def _params(n_axes, vmem_mib):
    return pltpu.CompilerParams(dimension_semantics=("arbitrary",) * n_axes,
                                vmem_limit_bytes=vmem_mib * MIB)


COL_CHUNK = 512


def _acc_dot(o_ref, a, w_ref):
    d = o_ref.shape[1]
    for c0 in range(0, d, COL_CHUNK):
        o_ref[:, c0:c0 + COL_CHUNK] += _dot(a, w_ref[:, c0:c0 + COL_CHUNK])


def _postnorm_residual(o_ref, x_ref, g_ref, scale):
    d = o_ref.shape[1]
    ss = jnp.zeros((o_ref.shape[0], 1), F32)
    for c0 in range(0, d, COL_CHUNK):
        a = o_ref[:, c0:c0 + COL_CHUNK]
        ss = ss + jnp.sum(a * a, axis=-1, keepdims=True)
    inv = scale * lax.rsqrt(ss * (1.0 / d) + EPS)
    for c0 in range(0, d, COL_CHUNK):
        cs = slice(c0, c0 + COL_CHUNK)
        o_ref[:, cs] = x_ref[:, cs] + (o_ref[:, cs] * inv) * g_ref[:, cs]


def _prenorm(xn_ref, x_ref, g_ref):
    d = x_ref.shape[1]
    ss = jnp.zeros((x_ref.shape[0], 1), F32)
    for c0 in range(0, d, COL_CHUNK):
        a = x_ref[:, c0:c0 + COL_CHUNK]
        ss = ss + jnp.sum(a * a, axis=-1, keepdims=True)
    inv = lax.rsqrt(ss * (1.0 / d) + EPS)
    for c0 in range(0, d, COL_CHUNK):
        cs = slice(c0, c0 + COL_CHUNK)
        xn_ref[:, cs] = ((x_ref[:, cs] * inv) * g_ref[:, cs]).astype(BF16)


def _ffn_body(has_tail, *refs):
    if has_tail:
        (x_ref, gpre_ref, gpost_ref, wg_ref, wu_ref, wd_ref, wgt_ref, wut_ref, wdt_ref,
         o_ref, xn_ref, h_ref) = refs
    else:
        (x_ref, gpre_ref, gpost_ref, wg_ref, wu_ref, wd_ref, o_ref, xn_ref, h_ref) = refs
    j = pl.program_id(1)
    nj = pl.num_programs(1) - 1

    @pl.when(j == 0)
    def _():
        _prenorm(xn_ref, x_ref, gpre_ref)
        o_ref[...] = jnp.zeros_like(o_ref)
        h_ref[...] = jnp.zeros_like(h_ref)

    def hidden(wg, wu):
        xn = xn_ref[...]
        return (_silu(_dot(xn, wg)) * _dot(xn, wu)).astype(BF16)

    @pl.when(j < nj)
    def _():
        _acc_dot(o_ref, h_ref[...], wd_ref)
        h_ref[...] = hidden(wg_ref[...], wu_ref[...])

    @pl.when(j == nj)
    def _():
        _acc_dot(o_ref, h_ref[...], wd_ref)
        if has_tail:
            wt = jnp.concatenate([wgt_ref[...].astype(BF16), wut_ref[...].astype(BF16)], axis=-1)
            gu = _dot(xn_ref[...], wt)
            ht = (_silu(gu[:, :LANE]) * gu[:, LANE:]).astype(BF16)
            _acc_dot(o_ref, ht, wdt_ref)
        _postnorm_residual(o_ref, x_ref, gpost_ref, 0.5)


def _ffn(h, gains, wg, wu, wd, layer, idx, k_pre, k_post, tm, tf):
    m, d = h.shape
    dff = wg.shape[-1]
    nj = dff // tf
    tail = dff - nj * tf
    assert m % tm == 0 and tail in (0, LANE) and nj >= 1
    tail_blk = dff // LANE - 1
    one = pl.Buffered(1)
    in_specs = [
        pl.BlockSpec((tm, d), lambda i, j: (i, 0)),
        pl.BlockSpec((None, None, 1, d), lambda i, j: (layer, k_pre, 0, 0)),
        pl.BlockSpec((None, None, 1, d), lambda i, j: (layer, k_post, 0, 0)),
        pl.BlockSpec((None, None, d, tf), lambda i, j: (layer, idx, 0, jnp.minimum(j, nj - 1))),
        pl.BlockSpec((None, None, d, tf), lambda i, j: (layer, idx, 0, jnp.minimum(j, nj - 1))),
        pl.BlockSpec((None, None, tf, d), lambda i, j: (layer, idx, jnp.maximum(j - 1, 0), 0)),
    ]
    args = [h, gains, gains, wg, wu, wd]
    if tail:
        in_specs += [
            pl.BlockSpec((None, None, d, LANE), lambda i, j: (layer, idx, 0, tail_blk), pipeline_mode=one),
            pl.BlockSpec((None, None, d, LANE), lambda i, j: (layer, idx, 0, tail_blk), pipeline_mode=one),
            pl.BlockSpec((None, None, LANE, d), lambda i, j: (layer, idx, tail_blk, 0), pipeline_mode=one),
        ]
        args += [wg, wu, wd]
    return pl.pallas_call(
        functools.partial(_ffn_body, bool(tail)),
        grid=(m // tm, nj + 1),
        in_specs=in_specs,
        out_specs=pl.BlockSpec((tm, d), lambda i, j: (i, 0)),
        out_shape=jax.ShapeDtypeStruct((m, d), F32),
        scratch_shapes=[pltpu.VMEM((tm, d), BF16), pltpu.VMEM((tm, tf), BF16)],
        compiler_params=_params(2, 61),
        name="ffn",
    )(*args)


def _nmm_body(x_ref, g_ref, w_ref, o_ref, xn_ref):
    @pl.when(pl.program_id(1) == 0)
    def _():
        _prenorm(xn_ref, x_ref, g_ref)

    o_ref[...] = _dot(xn_ref[...], w_ref[...])


def _norm_matmul(h, gains, layer, k_gain, w, j_w, tm, tn):
    m, d = h.shape
    n = w.shape[-1]
    assert m % tm == 0 and n % tn == 0
    return pl.pallas_call(
        _nmm_body,
        grid=(m // tm, n // tn),
        in_specs=[
            pl.BlockSpec((tm, d), lambda i, j: (i, 0), pipeline_mode=pl.Buffered(1)),
            pl.BlockSpec((None, None, 1, d), lambda i, j: (layer, k_gain, 0, 0)),
            pl.BlockSpec((None, d, tn), lambda i, j: (j_w, 0, j)),
        ],
        out_specs=pl.BlockSpec((tm, tn), lambda i, j: (i, j)),
        out_shape=jax.ShapeDtypeStruct((m, n), F32),
        scratch_shapes=[pltpu.VMEM((tm, d), BF16)],
        compiler_params=_params(2, 48),
        name="norm_matmul",
    )(h, gains, w)


def _mmpn_body(steps, *refs):
    n_in = len(steps)
    y_refs = refs[:n_in]
    w_ref, h_ref, g_ref, o_ref = refs[n_in:]
    k = pl.program_id(1)
    last = pl.num_programs(1) - 1

    @pl.when(k == 0)
    def _():
        o_ref[...] = jnp.zeros_like(o_ref)

    s0 = 0
    for y_ref, ns in zip(y_refs, steps):
        @pl.when((k >= s0) & (k < s0 + ns))
        def _(y_ref=y_ref):
            _acc_dot(o_ref, y_ref[...], w_ref)
        s0 += ns

    @pl.when(k == last)
    def _():
        _postnorm_residual(o_ref, h_ref, g_ref, 1.0)


def _matmul_postnorm(ys, w, j_w, h, gains, layer, k_gain, tm, tk):
    m, d = h.shape
    steps = tuple(y.shape[1] // tk for y in ys)
    assert all(y.shape[1] % tk == 0 for y in ys) and sum(steps) * tk == w.shape[1] and m % tm == 0
    in_specs = []
    s0 = 0
    for ns in steps:
        in_specs.append(pl.BlockSpec(
            (tm, tk), lambda i, k, s0=s0, ns=ns: (i, jnp.clip(k - s0, 0, ns - 1))))
        s0 += ns
    in_specs += [
        pl.BlockSpec((None, tk, d), lambda i, k: (j_w, k, 0)),
        pl.BlockSpec((tm, d), lambda i, k: (i, 0)),
        pl.BlockSpec((None, None, 1, d), lambda i, k: (layer, k_gain, 0, 0)),
    ]
    return pl.pallas_call(
        functools.partial(_mmpn_body, steps),
        grid=(m // tm, sum(steps)),
        in_specs=in_specs,
        out_specs=pl.BlockSpec((tm, d), lambda i, k: (i, 0)),
        out_shape=jax.ShapeDtypeStruct((m, d), F32),
        compiler_params=_params(2, 56),
        name="matmul_postnorm",
    )(*ys, w, h, gains)


def _s5_param_body(lre_ref, lim_ref, ls_ref, br_ref, bi_ref,
                   ap2r_ref, ap2i_ref, tabr_ref, tabi_ref, bbr_ref, bbi_ref):
    lr = jnp.minimum(lre_ref[...], -1e-4)
    li = lim_ref[...]
    step = jnp.exp(ls_ref[...])
    mag = jnp.exp(lr * step)
    ang = li * step
    ar = mag * jnp.cos(ang)
    ai = mag * jnp.sin(ang)
    den = lr * lr + li * li
    nr = ar - 1.0
    gr = (nr * lr + ai * li) / den
    gi = (ai * lr - nr * li) / den
    br = br_ref[...]
    bi = bi_ref[...]
    bbr_ref[...] = gr * br - gi * bi
    bbi_ref[...] = gr * bi + gi * br
    row = lax.broadcasted_iota(jnp.int32, (8, ar.shape[1]), 0)
    pr, pi = ar, ai
    tabr_ref[0:1, :] = ar
    tabi_ref[0:1, :] = ai
    for k in range(ap2r_ref.shape[0]):
        ap2r_ref[k:k + 1, :] = pr
        ap2i_ref[k:k + 1, :] = pi
        n = 2 ** k
        if n < 8:
            tr = tabr_ref[0:n, :]
            ti = tabi_ref[0:n, :]
            tabr_ref[n:2 * n, :] = tr * pr - ti * pi
            tabi_ref[n:2 * n, :] = tr * pi + ti * pr
            tabr_ref[8 * (k + 1):8 * (k + 2), :] = jnp.where(row >= n, pr, 0.0)
            tabi_ref[8 * (k + 1):8 * (k + 2), :] = jnp.where(row >= n, pi, 0.0)
        pr, pi = pr * pr - pi * pi, pr * pi + pi * pr


S5_TAB_ROWS = 32


def _s5_params(lam_re, lam_im, log_step, b_re, b_im):
    n = lam_re.shape[-1]
    c = b_re.shape[0]
    tt = S5_TAB_ROWS
    full = lambda shape: pl.BlockSpec(shape, lambda: (0,) * len(shape))
    return pl.pallas_call(
        _s5_param_body,
        in_specs=[full((1, n))] * 3 + [full((c, n))] * 2,
        out_specs=[full((8, n))] * 2 + [full((tt, n))] * 2 + [full((c, n))] * 2,
        out_shape=[jax.ShapeDtypeStruct((8, n), F32)] * 2 + [jax.ShapeDtypeStruct((tt, n), F32)] * 2
        + [jax.ShapeDtypeStruct((c, n), F32)] * 2,
        compiler_params=pltpu.CompilerParams(vmem_limit_bytes=32 * MIB),
        name="s5_params",
    )(lam_re, lam_im, log_step, b_re, b_im)


def _s5_body(scan, n_chunks, *refs):
    if scan:
        (u_ref, ap2r_ref, ap2i_ref, tabr_ref, tabi_ref, bbr_ref, bbi_ref, ccr_ref, cci_ref, d_ref,
         gw_ref, gb_ref, y_ref, sr_ref, si_ref, cr_s, ci_s) = refs
    else:
        (u_ref, ap2r_ref, ap2i_ref, bbr_ref, bbi_ref, ccr_ref, cci_ref, d_ref,
         gw_ref, gb_ref, h0r_ref, h0i_ref) = refs[:12]
        y_ref, sr_ref, si_ref = refs[-3:]
    tt = u_ref.shape[0]
    cw = u_ref.shape[1] // n_chunks
    sw = bbr_ref.shape[2]

    if scan:
        @pl.when(pl.program_id(1) == 0)
        def _():
            cr_s[...] = jnp.zeros_like(cr_s)
            ci_s[...] = jnp.zeros_like(ci_s)

    u = u_ref[...]
    ys = []
    for c in range(n_chunks):
        uc = u[:, c * cw:(c + 1) * cw]
        lanes = slice(c * sw, (c + 1) * sw)
        xr = _dot(uc, bbr_ref[c])
        xi = _dot(uc, bbi_ref[c])
        if scan:
            xr = xr.reshape(tt // 8, 8, sw)
            xi = xi.reshape(tt // 8, 8, sw)
            for k in range(3):
                pr = tabr_ref[8 * (k + 1):8 * (k + 2), lanes]
                pi = tabi_ref[8 * (k + 1):8 * (k + 2), lanes]
                sr = pltpu.roll(xr, 2 ** k, 1)
                si = pltpu.roll(xi, 2 ** k, 1)
                xr, xi = xr + (pr * sr - pi * si), xi + (pr * si + pi * sr)
            tr = tabr_ref[0:8, lanes]
            ti = tabi_ref[0:8, lanes]
            hr = cr_s[:, lanes]
            hi = ci_s[:, lanes]
            out_r, out_i = [], []
            for g in range(tt // 8):
                gr = xr[g] + (tr * hr - ti * hi)
                gi = xi[g] + (tr * hi + ti * hr)
                out_r.append(gr)
                out_i.append(gi)
                hr, hi = gr[7:8], gi[7:8]
            xr = jnp.concatenate(out_r, axis=0)
            xi = jnp.concatenate(out_i, axis=0)
            cr_s[:, lanes] = hr
            ci_s[:, lanes] = hi
        else:
            tr = ap2r_ref[0:1, lanes]
            ti = ap2i_ref[0:1, lanes]
            hr = h0r_ref[:, lanes]
            hi = h0i_ref[:, lanes]
            xr, xi = xr + (tr * hr - ti * hi), xi + (tr * hi + ti * hr)
            sr_ref[:, lanes] = xr
            si_ref[:, lanes] = xi
        ys.append(_dot(xr, ccr_ref[c]) - _dot(xi, cci_ref[c]) + d_ref[:, c * cw:(c + 1) * cw] * uc)
    y = _gelu_tanh(jnp.concatenate(ys, axis=-1))
    y_ref[...] = y * _sigmoid(_dot(y, gw_ref[...]) + gb_ref[...])
    if scan:
        sr_ref[0] = cr_s[...]
        si_ref[0] = ci_s[...]


def _s5_specs(j, width, n_chunks, cw, sw, n_state, nidx):
    zero = (0,) * nidx
    return dict(
        ap2=pl.BlockSpec((8, n_state), lambda *a: (0, 0)),
        bb=pl.BlockSpec((n_chunks, cw, sw), lambda *a: (0, 0, 0)),
        cc=pl.BlockSpec((n_chunks, sw, cw), lambda *a: (0, 0, 0)),
        d=pl.BlockSpec((None, 1, width), lambda *a: (j, 0, 0)),
        gw=pl.BlockSpec((None, width, width), lambda *a: (j, 0, 0)),
        gb=pl.BlockSpec((None, 1, width), lambda *a: (j, 0, 0)),
    )


def _s5_seq(proj, row0, bsz, t, tt, j, prm, d3, glu_w, glu_b3):
    ap2r, ap2i, tabr, tabi, bbr, bbi, ccr, cci = prm
    n_chunks, cw, sw = bbr.shape
    width = n_chunks * cw
    n_state = n_chunks * sw
    nt = t // tt
    rb0 = row0 // tt
    sp = _s5_specs(j, width, n_chunks, cw, sw, n_state, 2)
    tab = pl.BlockSpec((S5_TAB_ROWS, n_state), lambda b, i: (0, 0))
    st = pl.BlockSpec((1, 1, n_state), lambda b, i: (b, 0, 0))
    y, sr, si = pl.pallas_call(
        functools.partial(_s5_body, True, n_chunks),
        grid=(bsz, nt),
        in_specs=[pl.BlockSpec((tt, width), lambda b, i: (rb0 + b * nt + i, 0)),
                  sp["ap2"], sp["ap2"], tab, tab, sp["bb"], sp["bb"], sp["cc"], sp["cc"],
                  sp["d"], sp["gw"], sp["gb"]],
        out_specs=[pl.BlockSpec((tt, width), lambda b, i: (b * nt + i, 0)), st, st],
        out_shape=[jax.ShapeDtypeStruct((proj.shape[0], width), F32),
                   jax.ShapeDtypeStruct((bsz, 1, n_state), F32),
                   jax.ShapeDtypeStruct((bsz, 1, n_state), F32)],
        scratch_shapes=[pltpu.VMEM((1, n_state), F32)] * 2,
        compiler_params=_params(2, 48),
        name="s5_seq",
    )(proj, ap2r, ap2i, tabr, tabi, bbr, bbi, ccr, cci, d3, glu_w, glu_b3)
    return y, sr[:, 0], si[:, 0]


def _s5_step(proj, row0, bsz, j, prm, d3, glu_w, glu_b3, h0r, h0i, y_all):
    ap2r, ap2i, _, _, bbr, bbi, ccr, cci = prm
    n_chunks, cw, sw = bbr.shape
    width = n_chunks * cw
    n_state = n_chunks * sw
    rb0 = row0 // bsz
    sp = _s5_specs(j, width, n_chunks, cw, sw, n_state, 1)
    st = pl.BlockSpec((bsz, n_state), lambda i: (0, 0))
    in_specs = [pl.BlockSpec((bsz, width), lambda i: (rb0, 0)),
                sp["ap2"], sp["ap2"], sp["bb"], sp["bb"], sp["cc"], sp["cc"],
                sp["d"], sp["gw"], sp["gb"], st, st]
    args = [proj, ap2r, ap2i, bbr, bbi, ccr, cci, d3, glu_w, glu_b3, h0r, h0i]
    aliases = _shared_out(y_all, in_specs, args, 0)
    return pl.pallas_call(
        functools.partial(_s5_body, False, n_chunks),
        grid=(1,),
        in_specs=in_specs,
        out_specs=[pl.BlockSpec((bsz, width), lambda i: (rb0, 0)), st, st],
        out_shape=[jax.ShapeDtypeStruct(y_all.shape, F32),
                   jax.ShapeDtypeStruct((bsz, n_state), F32),
                   jax.ShapeDtypeStruct((bsz, n_state), F32)],
        input_output_aliases=aliases,
        compiler_params=_params(1, 48),
        name="s5_step",
    )(*args)


def _hgrn_lower_bound(logits, j, axis):
    n = logits.shape[axis]
    mx = jnp.max(logits, axis=axis, keepdims=True)
    e = jnp.exp(logits - mx)
    sm = e / jnp.sum(e, axis=axis, keepdims=True)
    pick = (lambda i: sm[i:i + 1, :]) if axis == 0 else (lambda i: sm[:, i:i + 1])
    cum = pick(0)
    for i in range(1, j + 1):
        cum = cum + pick(i)
    assert j < n
    return cum - pick(0)


def _hgrn_gates(fl, lb):
    logf = _logaddexp(_log_sigmoid(fl), jnp.log(lb) + _log_sigmoid(-fl))
    k = (1.0 - lb) * _sigmoid(-fl)
    return logf, k


def _hgrn_seq_body(j, chunk, sub, q_ref, f_ref, v_ref, gt_ref, lg_ref, gain_ref,
                   y_ref, s_ref, st_s, q_s, k_s, v_s, b_s, o_s):
    tt, dk = q_ref.shape
    i = pl.program_id(2)

    @pl.when(i == 0)
    def _():
        st_s[...] = jnp.zeros_like(st_s)

    lb = _hgrn_lower_bound(lg_ref[...], j, 0)
    logf, k = _hgrn_gates(f_ref[...], lb)
    rin = lax.broadcasted_iota(jnp.int32, (tt, dk), 0) & (chunk - 1)
    b = logf
    s = 1
    while s < chunk:
        b = b + jnp.where(rin >= s, _shift_rows(b, s), 0.0)
        s *= 2
    q_s[...] = _silu(q_ref[...])
    k_s[...] = k
    v_s[...] = v_ref[...]
    b_s[...] = b * LOG2_E
    nsub = chunk // sub
    tsub = lax.broadcasted_iota(jnp.int32, (sub, dk), 0)

    def one_chunk(c, carry):
        r0 = pl.multiple_of(c * chunk, chunk)
        qc = q_s[pl.ds(r0, chunk), :]
        kc = k_s[pl.ds(r0, chunk), :]
        vc = v_s[pl.ds(r0, chunk), :]
        bc = b_s[pl.ds(r0, chunk), :]
        acc = [jnp.zeros((sub, dk), F32) for _ in range(nsub)]
        for s_i in range(chunk):
            ks = kc[s_i:s_i + 1]
            bs = bc[s_i:s_i + 1]
            vs = vc[s_i:s_i + 1]
            sb = s_i // sub
            for tb in range(sb, nsub):
                diff = bc[tb * sub:(tb + 1) * sub] - bs
                if tb == sb:
                    diff = jnp.where(tsub >= s_i - sb * sub, diff, -jnp.inf)
                p = qc[tb * sub:(tb + 1) * sub] * ks * jnp.exp2(diff)
                acc[tb] = acc[tb] + jnp.sum(p, axis=-1, keepdims=True) * vs
        st = st_s[...]
        o = jnp.concatenate(acc, axis=0) + _dot_nt(qc * jnp.exp2(bc), st)
        b_last = bc[chunk - 1:chunk]
        st_s[...] = st * jnp.exp2(b_last) + _dot_tn(vc, kc * jnp.exp2(b_last - bc))
        o_s[pl.ds(r0, chunk), :] = o
        return carry

    lax.fori_loop(0, tt // chunk, one_chunk, 0, unroll=8)
    o = o_s[...]
    o = o * lax.rsqrt(jnp.mean(o * o, axis=-1, keepdims=True) + EPS)
    y_ref[...] = o * gain_ref[...] * _silu(gt_ref[...])

    @pl.when(i == pl.num_programs(2) - 1)
    def _():
        s_ref[...] = st_s[...].T


def _hgrn_seq(proj, row0, bsz, t, tt, j, col0, logits, gain3, chunk=32, sub=8):
    h, dk = HG_HEADS, LANE
    nt = t // tt
    rb0 = row0 // tt
    cb = col0 // dk
    sec = lambda n: pl.BlockSpec((tt, dk), lambda b, hh, i: (rb0 + b * nt + i, cb + n * h + hh))
    y, s = pl.pallas_call(
        functools.partial(_hgrn_seq_body, j, chunk, sub),
        grid=(bsz, h, nt),
        in_specs=[sec(0), sec(1), sec(2), sec(3),
                  pl.BlockSpec((logits.shape[0], dk), lambda b, hh, i: (0, hh)),
                  pl.BlockSpec((None, 1, dk), lambda b, hh, i: (j, 0, hh))],
        out_specs=[pl.BlockSpec((tt, dk), lambda b, hh, i: (b * nt + i, hh)),
                   pl.BlockSpec((None, None, dk, dk), lambda b, hh, i: (b, hh, 0, 0))],
        out_shape=[jax.ShapeDtypeStruct((proj.shape[0], h * dk), F32),
                   jax.ShapeDtypeStruct((bsz, h, dk, dk), F32)],
        scratch_shapes=[pltpu.VMEM((dk, dk), F32)] + [pltpu.VMEM((tt, dk), F32)] * 5,
        compiler_params=_params(3, 32),
        name="hgrn_seq",
    )(proj, proj, proj, proj, logits, gain3)
    return y, s


def _hgrn_step_body(j, bb, q_ref, f_ref, v_ref, gt_ref, qc_ref, fc_ref, lg_ref, lgt_ref, gain_ref,
                    s_ref, *rest):
    y_ref, so_ref = rest[-2:]
    lb_r = _hgrn_lower_bound(lg_ref[...], j, 0)
    lb_c = _hgrn_lower_bound(lgt_ref[...], j, 1)
    qcol = _silu(qc_ref[...])
    logf_c, kcol = _hgrn_gates(fc_ref[...], lb_c)
    fcol = jnp.exp(logf_c)
    for n in range(bb):
        s = s_ref[n]
        q_r = _silu(q_ref[n])
        _, k_r = _hgrn_gates(f_ref[n], lb_r)
        v_r = v_ref[n]
        q_c = qcol[:, n:n + 1]
        f_c = fcol[:, n:n + 1]
        k_c = kcol[:, n:n + 1]
        score = jnp.sum(q_r * k_r, axis=-1, keepdims=True)
        o = score * v_r + jnp.sum((q_c * f_c) * s, axis=0, keepdims=True)
        so_ref[n] = f_c * s + k_c * v_r
        o = o * lax.rsqrt(jnp.mean(o * o, axis=-1, keepdims=True) + EPS)
        y_ref[n:n + 1, :] = o * gain_ref[...] * _silu(gt_ref[n])


def _hgrn_step(proj3, proj_t, j, col0, logits, logits_t, gain3, s_all, s_prev, y_all, row0, bb=8):
    bsz = proj3.shape[0]
    h, dk = HG_HEADS, LANE
    cb = col0 // dk
    qt, ft = proj_t
    row = lambda n: pl.BlockSpec((bb, 1, dk), lambda hh, i: (i, 0, cb + n * h + hh))
    col = pl.BlockSpec((None, None, dk, bb), lambda hh, i: (hh, i, 0, 0))
    st = pl.BlockSpec((None, bb, None, dk, dk), lambda hh, i: (j, i, hh, 0, 0))
    in_specs = [row(0), row(1), row(2), row(3), col, col,
                pl.BlockSpec((logits.shape[0], dk), lambda hh, i: (0, hh)),
                pl.BlockSpec((dk, logits.shape[0]), lambda hh, i: (hh, 0)),
                pl.BlockSpec((None, 1, dk), lambda hh, i: (j, 0, hh)),
                st]
    args = [proj3, proj3, proj3, proj3, qt, ft, logits, logits_t, gain3, s_all]
    aliases = _shared_out(y_all, in_specs, args, 0)
    aliases.update(_shared_out(s_prev, in_specs, args, 1))
    rb0 = row0 // bb
    return pl.pallas_call(
        functools.partial(_hgrn_step_body, j, bb),
        grid=(h, bsz // bb),
        in_specs=in_specs,
        out_specs=[pl.BlockSpec((bb, dk), lambda hh, i: (rb0 + i, hh)), st],
        out_shape=[jax.ShapeDtypeStruct(y_all.shape, F32),
                   jax.ShapeDtypeStruct(s_all.shape, F32)],
        input_output_aliases=aliases,
        compiler_params=_params(2, 32),
        name="hgrn_step",
    )(*args)


def _ml_conv_seq(xm, cw_ref, cb_ref, stage_s, halo):
    tt = xm.shape[0]
    stage_s[0:8, :] = halo
    stage_s[8:, :] = xm
    acc = cb_ref[...] + cw_ref[ML_CONV - 1:ML_CONV, :] * xm
    for d in range(1, ML_CONV):
        acc = acc + cw_ref[ML_CONV - 1 - d:ML_CONV - d, :] * stage_s[8 - d:8 - d + tt, :]
    return _silu(acc)


def _ml_qkv(xc, xm, bdq_ref, bdk_ref, bdv_ref):
    nblk = bdq_ref.shape[0]
    w = bdq_ref.shape[1]
    xcb = xc.astype(BF16)
    xmb = xm.astype(BF16)
    q = jnp.concatenate([_dot(xcb[:, s * w:(s + 1) * w], bdq_ref[s]) for s in range(nblk)], axis=-1)
    k = jnp.concatenate([_dot(xcb[:, s * w:(s + 1) * w], bdk_ref[s]) for s in range(nblk)], axis=-1)
    v = jnp.concatenate([_dot(xmb[:, s * w:(s + 1) * w], bdv_ref[s]) for s in range(nblk)], axis=-1)
    return q, k, v


def _ml_gates_acc(c, q, k, v, wg_ref, bg_ref, g_ref):
    g = _dot(q, wg_ref[0]) + _dot(k, wg_ref[1]) + _dot(v, wg_ref[2])

    @pl.when(c == 0)
    def _():
        g_ref[...] = g + bg_ref[...]

    @pl.when(c > 0)
    def _():
        g_ref[...] += g


def _ml_pre_seq_body(xm_ref, cw_ref, cb_ref, bdq_ref, bdk_ref, bdv_ref, wg_ref, bg_ref,
                     g_ref, stage_s, halo_s):
    i = pl.program_id(1)
    c = pl.program_id(2)
    tt = xm_ref.shape[0]

    @pl.when(i == 0)
    def _():
        halo_s[c] = jnp.zeros(halo_s.shape[1:], F32)

    xm = xm_ref[...]
    xc = _ml_conv_seq(xm, cw_ref, cb_ref, stage_s, halo_s[c])
    halo_s[c] = xm[tt - 8:tt]
    q, k, v = _ml_qkv(xc, xm, bdq_ref, bdk_ref, bdv_ref)
    _ml_gates_acc(c, q, k, v, wg_ref, bg_ref, g_ref)


def _ml_pre_step_body(xm_ref, taps_ref, cw_ref, cb_ref, bdq_ref, bdk_ref, bdv_ref, wg_ref, bg_ref,
                      q_ref, k_ref, v_ref, xc_ref, g_ref):
    c = pl.program_id(0)
    xm = xm_ref[...]
    acc = cb_ref[...] + cw_ref[ML_CONV - 1:ML_CONV, :] * xm
    for d in range(1, ML_CONV):
        acc = acc + cw_ref[ML_CONV - 1 - d:ML_CONV - d, :] * taps_ref[ML_CONV - 1 - d]
    xc = _silu(acc)
    q, k, v = _ml_qkv(xc, xm, bdq_ref, bdk_ref, bdv_ref)
    q_ref[...] = q
    k_ref[...] = k
    v_ref[...] = v
    xc_ref[...] = xc
    _ml_gates_acc(c, q, k, v, wg_ref, bg_ref, g_ref)


def _ml_pre(up, row0, bsz, t, tt, lc, j, conv_w, conv_b3, bd, wg4, bg3, taps=None):
    inner = conv_w.shape[-1]
    ng = wg4.shape[-1]
    nc = inner // lc
    nblk = lc // MXU_DIM
    bdq, bdk, bdv = bd
    seq = taps is None
    rows = bsz * t
    if seq:
        nt = t // tt
        rb0 = row0 // tt
        grid = (bsz, nt, nc)
        rmap = lambda b, i, c: b * nt + i
        cmap = lambda b, i, c: c
    else:
        tt = bsz
        rb0 = row0 // tt
        grid = (nc,)
        rmap = lambda c: 0
        cmap = lambda c: c
    in_specs = [pl.BlockSpec((tt, lc), lambda *a: (rb0 + rmap(*a), cmap(*a)))]
    args = [up]
    if not seq:
        in_specs.append(pl.BlockSpec((ML_CONV - 1, tt, lc), lambda *a: (0, 0, cmap(*a))))
        args.append(taps)
    in_specs += [
        pl.BlockSpec((None, ML_CONV, lc), lambda *a: (j, 0, cmap(*a))),
        pl.BlockSpec((None, 1, lc), lambda *a: (j, 0, cmap(*a))),
        pl.BlockSpec((None, nblk, MXU_DIM, MXU_DIM), lambda *a: (j, cmap(*a), 0, 0)),
        pl.BlockSpec((None, nblk, MXU_DIM, MXU_DIM), lambda *a: (j, cmap(*a), 0, 0)),
        pl.BlockSpec((None, nblk, MXU_DIM, MXU_DIM), lambda *a: (j, cmap(*a), 0, 0)),
        pl.BlockSpec((None, 3, lc, ng), lambda *a: (j, 0, cmap(*a), 0)),
        pl.BlockSpec((None, 1, ng), lambda *a: (j, 0, 0)),
    ]
    args += [conv_w, conv_b3, bdq, bdk, bdv, wg4, bg3]
    big = pl.BlockSpec((tt, lc), lambda *a: (rmap(*a), cmap(*a)))
    n_big = 0 if seq else 4
    return pl.pallas_call(
        _ml_pre_seq_body if seq else _ml_pre_step_body,
        grid=grid,
        in_specs=in_specs,
        out_specs=[big] * n_big + [pl.BlockSpec((tt, ng), lambda *a: (rmap(*a), 0))],
        out_shape=[jax.ShapeDtypeStruct((rows, inner), F32)] * n_big + [jax.ShapeDtypeStruct((rows, ng), F32)],
        scratch_shapes=([pltpu.VMEM((tt + 8, lc), F32), pltpu.VMEM((nc, 8, lc), F32)] if seq else []),
        compiler_params=_params(len(grid), 48),
        name="ml_pre_seq" if seq else "ml_pre_step",
    )(*args)


def _ml_out(h, gain, skip, xc, z):
    mu = jnp.mean(h, axis=-1, keepdims=True)
    hc = h - mu
    hn = hc * lax.rsqrt(jnp.mean(hc * hc, axis=-1, keepdims=True) + EPS)
    return (hn * gain + skip * xc) * _silu(z)


def _ml_seq_body(xm_ref, z_ref, g_ref, gt_ref, cw_ref, cb_ref, bdq_ref, bdk_ref, bdv_ref, gain_ref, skip_ref,
                 o_ref, c_ref, n_ref, m_ref, stage_s, halo_s):
    hh = pl.program_id(1)
    i = pl.program_id(2)
    L, dh = xm_ref.shape
    nh = g_ref.shape[1] // 2

    @pl.when(i == 0)
    def _():
        c_ref[...] = jnp.zeros_like(c_ref)
        n_ref[...] = jnp.zeros_like(n_ref)
        m_ref[...] = jnp.zeros_like(m_ref)
        halo_s[...] = jnp.zeros_like(halo_s)

    xm = xm_ref[...]
    xc = _ml_conv_seq(xm, cw_ref, cb_ref, stage_s, halo_s[...])
    halo_s[...] = xm[L - 8:L]
    q, k, v = _ml_qkv(xc, xm, bdq_ref, bdk_ref, bdv_ref)
    k = k * (dh ** -0.5)

    g = g_ref[...]
    gt = gt_ref[...]
    lane = lax.broadcasted_iota(jnp.int32, g.shape, 1)
    sub = lax.broadcasted_iota(jnp.int32, gt.shape, 0)
    ig_c = jnp.sum(jnp.where(lane == hh, g, 0.0), axis=1, keepdims=True)
    lf_c = _log_sigmoid(jnp.sum(jnp.where(lane == nh + hh, g, 0.0), axis=1, keepdims=True))
    ig_r = jnp.sum(jnp.where(sub == hh, gt, 0.0), axis=0, keepdims=True)
    lf_r = _log_sigmoid(jnp.sum(jnp.where(sub == nh + hh, gt, 0.0), axis=0, keepdims=True))
    tr = lax.broadcasted_iota(jnp.int32, (L, L), 0)
    tc = lax.broadcasted_iota(jnp.int32, (L, L), 1)
    causal = tr >= tc
    b_c = jnp.sum(jnp.where(causal, lf_r, 0.0), axis=1, keepdims=True)
    b_r = jnp.sum(jnp.where(tr <= tc, lf_c, 0.0), axis=0, keepdims=True)
    m = m_ref[...]
    dlog = jnp.where(causal, b_c - b_r + ig_r, -jnp.inf)
    gsum = b_c + m
    mt = jnp.maximum(gsum, jnp.max(dlog, axis=1, keepdims=True))
    w = jnp.exp(dlog - mt)
    gi = jnp.exp(gsum - mt)
    c_old = c_ref[...]
    n_old = n_ref[...]
    qk = _dot_nt(q, k) * w
    num = _dot(qk, v) + gi * _dot(q, c_old)
    den = jnp.sum(qk, axis=1, keepdims=True) + gi * jnp.sum(q * n_old, axis=1, keepdims=True)
    h = num / jnp.maximum(jnp.abs(den), jnp.exp(-mt))
    m_new = mt[L - 1:L]
    b_last = b_c[L - 1:L]
    decay = jnp.exp(b_last + m - m_new)
    ks = k * jnp.exp(b_last - b_c + ig_c - m_new)
    c_ref[...] = decay * c_old + _dot_tn(ks, v)
    n_ref[...] = decay * n_old + jnp.sum(ks, axis=0, keepdims=True)
    m_ref[...] = m_new
    o_ref[...] = _ml_out(h, gain_ref[...], skip_ref[...], xc, z_ref[...])


def _ml_seq(up, row0, gates, gates_t, bsz, t, L, j, gain3, skip3, conv_w, conv_b3, bd):
    inner = conv_w.shape[-1]
    nh = ML_HEADS
    dh = inner // nh
    nt = t // L
    rb0 = row0 // L
    nblk = dh // MXU_DIM
    blk = pl.BlockSpec((L, dh), lambda b, hh, i: (b * nt + i, hh))
    par = pl.BlockSpec((None, 1, dh), lambda b, hh, i: (j, 0, hh))
    tile = pl.BlockSpec((None, nblk, MXU_DIM, MXU_DIM), lambda b, hh, i: (0, hh, 0, 0))
    bdq, bdk, bdv = bd
    return pl.pallas_call(
        _ml_seq_body,
        grid=(bsz, nh, nt),
        in_specs=[pl.BlockSpec((L, dh), lambda b, hh, i: (rb0 + b * nt + i, hh)),
                  pl.BlockSpec((L, dh), lambda b, hh, i: (rb0 + b * nt + i, nh + hh)),
                  pl.BlockSpec((L, 2 * nh), lambda b, hh, i: (b * nt + i, 0)),
                  pl.BlockSpec((2 * nh, L), lambda b, hh, i: (0, b * nt + i)),
                  pl.BlockSpec((None, ML_CONV, dh), lambda b, hh, i: (0, 0, hh)),
                  pl.BlockSpec((None, 1, dh), lambda b, hh, i: (0, 0, hh)),
                  tile, tile, tile, par, par],
        out_specs=[blk,
                   pl.BlockSpec((None, None, dh, dh), lambda b, hh, i: (b, hh, 0, 0)),
                   pl.BlockSpec((None, None, 1, dh), lambda b, hh, i: (b, hh, 0, 0)),
                   pl.BlockSpec((None, None, 1, 1), lambda b, hh, i: (b, hh, 0, 0))],
        out_shape=[jax.ShapeDtypeStruct((up.shape[0], inner), F32),
                   jax.ShapeDtypeStruct((bsz, nh, dh, dh), F32),
                   jax.ShapeDtypeStruct((bsz, nh, 1, dh), F32),
                   jax.ShapeDtypeStruct((bsz, nh, 1, 1), F32)],
        scratch_shapes=[pltpu.VMEM((L + 8, dh), F32), pltpu.VMEM((8, dh), F32)],
        compiler_params=_params(3, 48),
        name="ml_seq",
    )(up, up, gates, gates_t, conv_w, conv_b3, bdq, bdk, bdv, gain3, skip3)


def _ml_step_body(bb, q_ref, k_ref, v_ref, xc_ref, z_ref, qc_ref, kc_ref, ig_ref, lf_ref, m_ref,
                  gain_ref, skip_ref, c_ref, n_ref, *rest):
    o_ref, co_ref, no_ref, mo_ref = rest[-4:]
    dh = q_ref.shape[-1]
    scale = dh ** -0.5
    ig_all = ig_ref[...]
    lf_all = _log_sigmoid(lf_ref[...])
    m_all = m_ref[...]
    g_all = lf_all + m_all
    mt_all = jnp.maximum(g_all, ig_all)
    w_all = jnp.exp(ig_all - mt_all)
    gi_all = jnp.exp(g_all - mt_all)
    mo_ref[...] = mt_all
    for n in range(bb):
        c_old = c_ref[n]
        n_old = n_ref[n]
        q_r = q_ref[n]
        k_r = k_ref[n] * scale
        v_r = v_ref[n]
        q_c = qc_ref[:, n:n + 1]
        k_c = kc_ref[:, n:n + 1] * scale
        mt = mt_all[:, n:n + 1]
        w = w_all[:, n:n + 1]
        gi = gi_all[:, n:n + 1]
        qk = jnp.sum(q_r * k_r, axis=-1, keepdims=True) * w
        num = qk * v_r + gi * jnp.sum(q_c * c_old, axis=0, keepdims=True)
        den = qk + gi * jnp.sum(q_r * n_old, axis=-1, keepdims=True)
        h = num / jnp.maximum(jnp.abs(den), jnp.exp(-mt))
        co_ref[n] = gi * c_old + (w * k_c) * v_r
        no_ref[n] = gi * n_old + w * k_r
        o_ref[n:n + 1, :] = _ml_out(h, gain_ref[...], skip_ref[...], xc_ref[n], z_ref[n])


def _ml_step(q3, k3, v3, xc3, up3, q_t, k_t, ig_t, lf_t, m_t, j, gain3, skip3, c_all, n_all, c_prev,
             y_all, row0, bb=8):
    bsz, _, inner = q3.shape
    nh = ML_HEADS
    dh = inner // nh
    row = pl.BlockSpec((bb, 1, dh), lambda hh, i: (i, 0, hh))
    col = pl.BlockSpec((None, None, dh, bb), lambda hh, i: (hh, i, 0, 0))
    sc = pl.BlockSpec((None, None, 1, bb), lambda hh, i: (hh, i, 0, 0))
    par = pl.BlockSpec((None, 1, dh), lambda hh, i: (j, 0, hh))
    nst = pl.BlockSpec((bb, None, 1, dh), lambda hh, i: (i, hh, 0, 0))
    cst = pl.BlockSpec((None, bb, None, dh, dh), lambda hh, i: (j, i, hh, 0, 0))
    nst_in = pl.BlockSpec((None, bb, None, 1, dh), lambda hh, i: (j, i, hh, 0, 0))
    in_specs = [row, row, row, row,
                pl.BlockSpec((bb, 1, dh), lambda hh, i: (i, 0, nh + hh)),
                col, col, sc, sc, sc, par, par, cst, nst_in]
    args = [q3, k3, v3, xc3, up3, q_t, k_t, ig_t, lf_t, m_t, gain3, skip3, c_all, n_all]
    aliases = _shared_out(y_all, in_specs, args, 0)
    aliases.update(_shared_out(c_prev, in_specs, args, 1))
    rb0 = row0 // bb
    return pl.pallas_call(
        functools.partial(_ml_step_body, bb),
        grid=(nh, bsz // bb),
        in_specs=in_specs,
        out_specs=[pl.BlockSpec((bb, dh), lambda hh, i: (rb0 + i, hh)), cst, nst, sc],
        out_shape=[jax.ShapeDtypeStruct(y_all.shape, F32),
                   jax.ShapeDtypeStruct(c_all.shape, F32),
                   jax.ShapeDtypeStruct(n_all.shape[1:], F32),
                   jax.ShapeDtypeStruct(m_t.shape, F32)],
        input_output_aliases=aliases,
        compiler_params=_params(2, 56),
        name="ml_step",
    )(*args)


def _block_diag(w, size):
    n, c, _ = w.shape
    per = size // c
    w4 = w.reshape(n // per, per, c, c)
    eye = jnp.eye(per, dtype=w.dtype)
    out = w4[:, :, :, None, :] * eye[None, :, None, :, None]
    return out.reshape(n // per, size, size).astype(BF16)


def _to_cols(x, nh, bb):
    bsz, w = x.shape
    d = w // nh
    return x.reshape(bsz // bb, bb, nh, d).transpose(2, 0, 3, 1)


def _s5_layout(bbt_r, bbt_i, c_re, c_im, n_chunks):
    ch, n_state = bbt_r.shape
    g = c_re.shape[0]
    p = n_state // g
    per = g // n_chunks
    eye = jnp.eye(per, dtype=F32)

    def bmat(bt):
        b4 = bt.reshape(ch, n_chunks, per, p).transpose(1, 0, 2, 3)
        out = eye[None, :, None, :, None] * b4[:, None, :, :, :]
        return out.reshape(n_chunks, per * ch, per * p).astype(BF16)

    def cmat(c):
        c4 = c.reshape(n_chunks, per, ch, p).transpose(0, 1, 3, 2)
        out = c4[:, :, :, None, :] * eye[None, :, None, :, None]
        return out.reshape(n_chunks, per * p, per * ch).astype(BF16)

    return bmat(bbt_r), bmat(bbt_i), cmat(c_re), cmat(c_im)


TM_FFN = 1040
TF_FFN = 256
TM_PROJ = 2080
TN_PROJ = 512
TM_OUT = 1040
TK_OUT = 512
TT_S5 = 256
TT_HGRN = 1024
TT_MLPRE = 512
LC_MLPRE = 512
L_ML = 256
S5_CHUNKS = 8


def kernel(x_prompt, x_sample, state_s5_re, state_s5_im, state_hgrn, state_conv, state_mlstm_c, state_mlstm_n, state_mlstm_m, norm_gain, ffn_w_gate, ffn_w_up, ffn_w_down, ab_w_in, ab_w_out, s5_lambda_re, s5_lambda_im, s5_log_step, s5_b_re, s5_b_im, s5_c_re, s5_c_im, s5_d, s5_glu_w, s5_glu_b, hgrn_lb_logits, hgrn_norm_gain, ml_w_up, ml_conv_w, ml_conv_b, ml_w_q, ml_w_k, ml_w_v, ml_w_gates, ml_b_gates, ml_norm_gain, ml_skip, ml_w_down):
    bp, t, d = x_prompt.shape
    bs = x_sample.shape[0]
    depth = norm_gain.shape[0]
    mp = bp * t
    n_ab, g5, p5 = s5_lambda_re.shape
    n_state = g5 * p5
    s5_w = g5 * S5_GROUP
    hg_w = HG_HEADS * LANE
    inner = ml_conv_w.shape[-1]
    n_c = ml_conv_w.shape[0]
    dh = inner // ML_HEADS

    gains = norm_gain.reshape(depth, norm_gain.shape[1], 1, d)
    h = jnp.concatenate([x_prompt.reshape(mp, d), x_sample.reshape(bs, d)], axis=0)

    out = dict(p_s5r=[], p_s5i=[], p_hg=[], p_conv=[], p_c=[], p_n=[], p_m=[],
               s_s5r=[], s_s5i=[], s_conv=[], s_n=[], s_m=[])
    s_hg = s_c = None

    for layer in range(depth):
        j = layer // 2
        h = _ffn(h, gains, ffn_w_gate, ffn_w_up, ffn_w_down, layer, 0, 0, 1, TM_FFN, TF_FFN)
        if layer % 2 == 0:
            proj = _norm_matmul(h, gains, layer, 2, ab_w_in, j, TM_PROJ, TN_PROJ)
            prm = _s5_params(s5_lambda_re[j].reshape(1, n_state), s5_lambda_im[j].reshape(1, n_state),
                             jnp.repeat(s5_log_step[j], p5).reshape(1, n_state),
                             s5_b_re[j].transpose(2, 0, 1).reshape(S5_GROUP, n_state),
                             s5_b_im[j].transpose(2, 0, 1).reshape(S5_GROUP, n_state))
            ap2r, ap2i, tabr, tabi, bbt_r, bbt_i = prm
            bbr, bbi, ccr, cci = _s5_layout(bbt_r, bbt_i, s5_c_re[j], s5_c_im[j], S5_CHUNKS)
            s5p = (ap2r, ap2i, tabr, tabi, bbr, bbi, ccr, cci)
            d3 = s5_d.reshape(n_ab, 1, s5_w)
            gb3 = s5_glu_b.reshape(n_ab, 1, s5_w)
            ya, sr_p, si_p = _s5_seq(proj, 0, bp, t, TT_S5, j, s5p, d3, s5_glu_w, gb3)
            ya, sr_s, si_s = _s5_step(proj, mp, bs, j, s5p, d3, s5_glu_w, gb3,
                                      state_s5_re[j].reshape(bs, n_state), state_s5_im[j].reshape(bs, n_state), ya)
            out["p_s5r"].append(sr_p.reshape(bp, g5, p5))
            out["p_s5i"].append(si_p.reshape(bp, g5, p5))
            out["s_s5r"].append(sr_s.reshape(bs, g5, p5))
            out["s_s5i"].append(si_s.reshape(bs, g5, p5))
            hgain3 = hgrn_norm_gain.reshape(n_ab, 1, hg_w)
            yb, hg_p = _hgrn_seq(proj, 0, bp, t, TT_HGRN, j, s5_w, hgrn_lb_logits, hgain3)
            proj_s = proj[mp:]
            bb_h = 8
            yb, s_hg = _hgrn_step(
                proj_s.reshape(bs, 1, proj.shape[1]),
                (_to_cols(proj_s[:, s5_w:s5_w + hg_w], HG_HEADS, bb_h),
                 _to_cols(proj_s[:, s5_w + hg_w:s5_w + 2 * hg_w], HG_HEADS, bb_h)),
                j, s5_w, hgrn_lb_logits, hgrn_lb_logits.T, hgain3, state_hgrn, s_hg, yb, mp, bb_h)
            out["p_hg"].append(hg_p)
            h = _matmul_postnorm([ya, yb], ab_w_out, j, h, gains, layer, 3, TM_OUT, TK_OUT)
        else:
            up = _norm_matmul(h, gains, layer, 2, ml_w_up, j, TM_PROJ, TN_PROJ)
            bd = (_block_diag(ml_w_q[j], MXU_DIM)[None], _block_diag(ml_w_k[j], MXU_DIM)[None],
                  _block_diag(ml_w_v[j], MXU_DIM)[None])
            wg4 = ml_w_gates[j].reshape(1, 3, inner, 2 * ML_HEADS)
            bg3 = ml_b_gates[j].reshape(1, 1, 2 * ML_HEADS)
            cw = ml_conv_w[j][None]
            cb3 = ml_conv_b[j].reshape(1, 1, inner)
            gain3 = ml_norm_gain.reshape(n_c, 1, inner)
            skip3 = ml_skip.reshape(n_c, 1, inner)
            (g_p,) = _ml_pre(up, 0, bp, t, TT_MLPRE, LC_MLPRE, 0, cw, cb3, bd, wg4, bg3)
            hf, c_p, n_p, m_p = _ml_seq(up, 0, g_p, g_p.T, bp, t, L_ML, j, gain3, skip3, cw, cb3, bd)
            out["p_conv"].append(jnp.stack([
                lax.slice(up, ((b + 1) * t - (ML_CONV - 1), 0), ((b + 1) * t, inner)) for b in range(bp)]))
            out["p_c"].append(c_p)
            out["p_n"].append(n_p.reshape(bp, ML_HEADS, dh))
            out["p_m"].append(m_p.reshape(bp, ML_HEADS))
            taps = state_conv[j].transpose(1, 0, 2)
            q_s, k_s, v_s, xc_s, g_s = _ml_pre(up, mp, bs, 1, None, LC_MLPRE, 0, cw, cb3, bd, wg4, bg3, taps=taps)
            bb_m = 8
            sc = lambda a: a.T.reshape(ML_HEADS, bs // bb_m, 1, bb_m)
            r3 = lambda a: a.reshape(bs, 1, a.shape[-1])
            hf, s_c, n_s, m_s = _ml_step(
                r3(q_s), r3(k_s), r3(v_s), r3(xc_s), r3(up[mp:]),
                _to_cols(q_s, ML_HEADS, bb_m), _to_cols(k_s, ML_HEADS, bb_m),
                sc(g_s[:, :ML_HEADS]), sc(g_s[:, ML_HEADS:]), sc(state_mlstm_m[j]),
                j, gain3, skip3, state_mlstm_c, state_mlstm_n.reshape(n_c, bs, ML_HEADS, 1, dh), s_c,
                hf, mp, bb_m)
            out["s_conv"].append(jnp.concatenate([state_conv[j][:, 1:], up[mp:, None, :inner]], axis=1))
            out["s_n"].append(n_s.reshape(bs, ML_HEADS, dh))
            out["s_m"].append(m_s.reshape(ML_HEADS, bs).T)
            h = _matmul_postnorm([hf], ml_w_down, j, h, gains, layer, 3, TM_OUT, TK_OUT)
        h = _ffn(h, gains, ffn_w_gate, ffn_w_up, ffn_w_down, layer, 1, 4, 5, TM_FFN, TF_FFN)

    st = lambda name: jnp.stack(out[name])
    return (h[:mp].reshape(bp, t, d), h[mp:].reshape(bs, 1, d),
            st("p_s5r"), st("p_s5i"), st("p_hg"), st("p_conv"), st("p_c"), st("p_n"), st("p_m"),
            st("s_s5r"), st("s_s5i"), s_hg, st("s_conv"), s_c, st("s_n"), st("s_m"))
```

```python
import functools
import math

import jax
import jax.numpy as jnp
from jax import lax
from jax.experimental import pallas as pl
from jax.experimental.pallas import tpu as pltpu

F32 = jnp.float32
BF16 = jnp.bfloat16
EPS = 1e-6
MIB = 1024 * 1024

S5_GROUP = 16
S5_STATE = 64
HG_HEADS = 8
ML_HEADS = 8
ML_CONV = 4
ML_BLOCK = 4

LOG2_E = 1.4426950408889634

LANE = 128
MXU_DIM = 256


def _dot(a, b):
    return jnp.dot(a.astype(BF16), b.astype(BF16), preferred_element_type=F32)


def _dot_nt(a, b):
    return lax.dot_general(a.astype(BF16), b.astype(BF16), (((1,), (1,)), ((), ())),
                           preferred_element_type=F32)


def _dot_tn(a, b):
    return lax.dot_general(a.astype(BF16), b.astype(BF16), (((0,), (0,)), ((), ())),
                           preferred_element_type=F32)


def _sigmoid(x):
    return 1.0 / (1.0 + jnp.exp(-x))


def _silu(x):
    return x * _sigmoid(x)


def _log_sigmoid(x):
    return jnp.minimum(x, 0.0) - jnp.log1p(jnp.exp(-jnp.abs(x)))


def _logaddexp(a, b):
    return jnp.maximum(a, b) + jnp.log1p(jnp.exp(-jnp.abs(a - b)))


def _gelu_tanh(x):
    return x * (0.5 * (1.0 + jnp.tanh(math.sqrt(2.0 / math.pi) * (x + 0.044715 * (x * x * x)))))


def _shift_rows(x, s):
    n = x.shape[0]
    if s % 8 == 0:
        return jnp.concatenate([jnp.zeros((s,) + x.shape[1:], x.dtype), x[:n - s]], axis=0)
    row = lax.broadcasted_iota(jnp.int32, x.shape, 0)
    return jnp.where(row >= s, pltpu.roll(x, s, 0), 0.0)


def _shared_out(prev, in_specs, args, out_index):
    if prev is None:
        return {}
    in_specs.append(pl.BlockSpec(memory_space=pl.ANY))
    args.append(prev)
    return {len(args) - 1: out_index}


def _params(n_axes, vmem_mib):
    return pltpu.CompilerParams(dimension_semantics=("arbitrary",) * n_axes,
                                vmem_limit_bytes=vmem_mib * MIB)


COL_CHUNK = 512


def _acc_dot(o_ref, a, w_ref):
    d = o_ref.shape[1]
    for c0 in range(0, d, COL_CHUNK):
        o_ref[:, c0:c0 + COL_CHUNK] += _dot(a, w_ref[:, c0:c0 + COL_CHUNK])


def _postnorm_residual(o_ref, x_ref, g_ref, scale):
    d = o_ref.shape[1]
    ss = jnp.zeros((o_ref.shape[0], 1), F32)
    for c0 in range(0, d, COL_CHUNK):
        a = o_ref[:, c0:c0 + COL_CHUNK]
        ss = ss + jnp.sum(a * a, axis=-1, keepdims=True)
    inv = scale * lax.rsqrt(ss * (1.0 / d) + EPS)
    for c0 in range(0, d, COL_CHUNK):
        cs = slice(c0, c0 + COL_CHUNK)
        o_ref[:, cs] = x_ref[:, cs] + (o_ref[:, cs] * inv) * g_ref[:, cs]


def _prenorm(xn_ref, x_ref, g_ref):
    d = x_ref.shape[1]
    ss = jnp.zeros((x_ref.shape[0], 1), F32)
    for c0 in range(0, d, COL_CHUNK):
        a = x_ref[:, c0:c0 + COL_CHUNK]
        ss = ss + jnp.sum(a * a, axis=-1, keepdims=True)
    inv = lax.rsqrt(ss * (1.0 / d) + EPS)
    for c0 in range(0, d, COL_CHUNK):
        cs = slice(c0, c0 + COL_CHUNK)
        xn_ref[:, cs] = ((x_ref[:, cs] * inv) * g_ref[:, cs]).astype(BF16)


def _ffn_body(has_tail, *refs):
    if has_tail:
        (x_ref, gpre_ref, gpost_ref, wg_ref, wu_ref, wd_ref, wgt_ref, wut_ref, wdt_ref,
         o_ref, xn_ref, h_ref) = refs
    else:
        (x_ref, gpre_ref, gpost_ref, wg_ref, wu_ref, wd_ref, o_ref, xn_ref, h_ref) = refs
    j = pl.program_id(1)
    nj = pl.num_programs(1) - 1

    @pl.when(j == 0)
    def _():
        _prenorm(xn_ref, x_ref, gpre_ref)
        o_ref[...] = jnp.zeros_like(o_ref)
        h_ref[...] = jnp.zeros_like(h_ref)

    def hidden(wg, wu):
        xn = xn_ref[...]
        return (_silu(_dot(xn, wg)) * _dot(xn, wu)).astype(BF16)

    @pl.when(j < nj)
    def _():
        _acc_dot(o_ref, h_ref[...], wd_ref)
        h_ref[...] = hidden(wg_ref[...], wu_ref[...])

    @pl.when(j == nj)
    def _():
        _acc_dot(o_ref, h_ref[...], wd_ref)
        if has_tail:
            wt = jnp.concatenate([wgt_ref[...].astype(BF16), wut_ref[...].astype(BF16)], axis=-1)
            gu = _dot(xn_ref[...], wt)
            ht = (_silu(gu[:, :LANE]) * gu[:, LANE:]).astype(BF16)
            _acc_dot(o_ref, ht, wdt_ref)
        _postnorm_residual(o_ref, x_ref, gpost_ref, 0.5)


def _ffn(h, gains, wg, wu, wd, layer, idx, k_pre, k_post, tm, tf):
    m, d = h.shape
    dff = wg.shape[-1]
    nj = dff // tf
    tail = dff - nj * tf
    assert m % tm == 0 and tail in (0, LANE) and nj >= 1
    tail_blk = dff // LANE - 1
    one = pl.Buffered(1)
    in_specs = [
        pl.BlockSpec((tm, d), lambda i, j: (i, 0)),
        pl.BlockSpec((None, None, 1, d), lambda i, j: (layer, k_pre, 0, 0)),
        pl.BlockSpec((None, None, 1, d), lambda i, j: (layer, k_post, 0, 0)),
        pl.BlockSpec((None, None, d, tf), lambda i, j: (layer, idx, 0, jnp.minimum(j, nj - 1))),
        pl.BlockSpec((None, None, d, tf), lambda i, j: (layer, idx, 0, jnp.minimum(j, nj - 1))),
        pl.BlockSpec((None, None, tf, d), lambda i, j: (layer, idx, jnp.maximum(j - 1, 0), 0)),
    ]
    args = [h, gains, gains, wg, wu, wd]
    if tail:
        in_specs += [
            pl.BlockSpec((None, None, d, LANE), lambda i, j: (layer, idx, 0, tail_blk), pipeline_mode=one),
            pl.BlockSpec((None, None, d, LANE), lambda i, j: (layer, idx, 0, tail_blk), pipeline_mode=one),
            pl.BlockSpec((None, None, LANE, d), lambda i, j: (layer, idx, tail_blk, 0), pipeline_mode=one),
        ]
        args += [wg, wu, wd]
    return pl.pallas_call(
        functools.partial(_ffn_body, bool(tail)),
        grid=(m // tm, nj + 1),
        in_specs=in_specs,
        out_specs=pl.BlockSpec((tm, d), lambda i, j: (i, 0)),
        out_shape=jax.ShapeDtypeStruct((m, d), F32),
        scratch_shapes=[pltpu.VMEM((tm, d), BF16), pltpu.VMEM((tm, tf), BF16)],
        compiler_params=_params(2, 61),
        name="ffn",
    )(*args)


def _nmm_body(x_ref, g_ref, w_ref, o_ref, xn_ref):
    @pl.when(pl.program_id(1) == 0)
    def _():
        _prenorm(xn_ref, x_ref, g_ref)

    o_ref[...] = _dot(xn_ref[...], w_ref[...])


def _norm_matmul(h, gains, layer, k_gain, w, j_w, tm, tn):
    m, d = h.shape
    n = w.shape[-1]
    assert m % tm == 0 and n % tn == 0
    return pl.pallas_call(
        _nmm_body,
        grid=(m // tm, n // tn),
        in_specs=[
            pl.BlockSpec((tm, d), lambda i, j: (i, 0), pipeline_mode=pl.Buffered(1)),
            pl.BlockSpec((None, None, 1, d), lambda i, j: (layer, k_gain, 0, 0)),
            pl.BlockSpec((None, d, tn), lambda i, j: (j_w, 0, j)),
        ],
        out_specs=pl.BlockSpec((tm, tn), lambda i, j: (i, j)),
        out_shape=jax.ShapeDtypeStruct((m, n), F32),
        scratch_shapes=[pltpu.VMEM((tm, d), BF16)],
        compiler_params=_params(2, 48),
        name="norm_matmul",
    )(h, gains, w)


def _mmpn_body(steps, *refs):
    n_in = len(steps)
    y_refs = refs[:n_in]
    w_ref, h_ref, g_ref, o_ref = refs[n_in:]
    k = pl.program_id(1)
    last = pl.num_programs(1) - 1

    @pl.when(k == 0)
    def _():
        o_ref[...] = jnp.zeros_like(o_ref)

    s0 = 0
    for y_ref, ns in zip(y_refs, steps):
        @pl.when((k >= s0) & (k < s0 + ns))
        def _(y_ref=y_ref):
            _acc_dot(o_ref, y_ref[...], w_ref)
        s0 += ns

    @pl.when(k == last)
    def _():
        _postnorm_residual(o_ref, h_ref, g_ref, 1.0)


def _matmul_postnorm(ys, w, j_w, h, gains, layer, k_gain, tm, tk):
    m, d = h.shape
    steps = tuple(y.shape[1] // tk for y in ys)
    assert all(y.shape[1] % tk == 0 for y in ys) and sum(steps) * tk == w.shape[1] and m % tm == 0
    in_specs = []
    s0 = 0
    for ns in steps:
        in_specs.append(pl.BlockSpec(
            (tm, tk), lambda i, k, s0=s0, ns=ns: (i, jnp.clip(k - s0, 0, ns - 1))))
        s0 += ns
    in_specs += [
        pl.BlockSpec((None, tk, d), lambda i, k: (j_w, k, 0)),
        pl.BlockSpec((tm, d), lambda i, k: (i, 0)),
        pl.BlockSpec((None, None, 1, d), lambda i, k: (layer, k_gain, 0, 0)),
    ]
    return pl.pallas_call(
        functools.partial(_mmpn_body, steps),
        grid=(m // tm, sum(steps)),
        in_specs=in_specs,
        out_specs=pl.BlockSpec((tm, d), lambda i, k: (i, 0)),
        out_shape=jax.ShapeDtypeStruct((m, d), F32),
        compiler_params=_params(2, 56),
        name="matmul_postnorm",
    )(*ys, w, h, gains)


def _s5_param_body(lre_ref, lim_ref, ls_ref, br_ref, bi_ref,
                   ap2r_ref, ap2i_ref, tabr_ref, tabi_ref, bbr_ref, bbi_ref):
    lr = jnp.minimum(lre_ref[...], -1e-4)
    li = lim_ref[...]
    step = jnp.exp(ls_ref[...])
    mag = jnp.exp(lr * step)
    ang = li * step
    ar = mag * jnp.cos(ang)
    ai = mag * jnp.sin(ang)
    den = lr * lr + li * li
    nr = ar - 1.0
    gr = (nr * lr + ai * li) / den
    gi = (ai * lr - nr * li) / den
    br = br_ref[...]
    bi = bi_ref[...]
    bbr_ref[...] = gr * br - gi * bi
    bbi_ref[...] = gr * bi + gi * br
    row = lax.broadcasted_iota(jnp.int32, (8, ar.shape[1]), 0)
    pr, pi = ar, ai
    tabr_ref[0:1, :] = ar
    tabi_ref[0:1, :] = ai
    for k in range(ap2r_ref.shape[0]):
        ap2r_ref[k:k + 1, :] = pr
        ap2i_ref[k:k + 1, :] = pi
        n = 2 ** k
        if n < 8:
            tr = tabr_ref[0:n, :]
            ti = tabi_ref[0:n, :]
            tabr_ref[n:2 * n, :] = tr * pr - ti * pi
            tabi_ref[n:2 * n, :] = tr * pi + ti * pr
            tabr_ref[8 * (k + 1):8 * (k + 2), :] = jnp.where(row >= n, pr, 0.0)
            tabi_ref[8 * (k + 1):8 * (k + 2), :] = jnp.where(row >= n, pi, 0.0)
        pr, pi = pr * pr - pi * pi, pr * pi + pi * pr


S5_TAB_ROWS = 32


def _s5_params(lam_re, lam_im, log_step, b_re, b_im):
    n = lam_re.shape[-1]
    c = b_re.shape[0]
    tt = S5_TAB_ROWS
    full = lambda shape: pl.BlockSpec(shape, lambda: (0,) * len(shape))
    return pl.pallas_call(
        _s5_param_body,
        in_specs=[full((1, n))] * 3 + [full((c, n))] * 2,
        out_specs=[full((8, n))] * 2 + [full((tt, n))] * 2 + [full((c, n))] * 2,
        out_shape=[jax.ShapeDtypeStruct((8, n), F32)] * 2 + [jax.ShapeDtypeStruct((tt, n), F32)] * 2
        + [jax.ShapeDtypeStruct((c, n), F32)] * 2,
        compiler_params=pltpu.CompilerParams(vmem_limit_bytes=32 * MIB),
        name="s5_params",
    )(lam_re, lam_im, log_step, b_re, b_im)


def _s5_body(scan, n_chunks, *refs):
    if scan:
        (u_ref, ap2r_ref, ap2i_ref, tabr_ref, tabi_ref, bbr_ref, bbi_ref, ccr_ref, cci_ref, d_ref,
         gw_ref, gb_ref, y_ref, sr_ref, si_ref, cr_s, ci_s) = refs
    else:
        (u_ref, ap2r_ref, ap2i_ref, bbr_ref, bbi_ref, ccr_ref, cci_ref, d_ref,
         gw_ref, gb_ref, h0r_ref, h0i_ref) = refs[:12]
        y_ref, sr_ref, si_ref = refs[-3:]
    tt = u_ref.shape[0]
    cw = u_ref.shape[1] // n_chunks
    sw = bbr_ref.shape[2]

    if scan:
        @pl.when(pl.program_id(1) == 0)
        def _():
            cr_s[...] = jnp.zeros_like(cr_s)
            ci_s[...] = jnp.zeros_like(ci_s)

    u = u_ref[...]
    ys = []
    for c in range(n_chunks):
        uc = u[:, c * cw:(c + 1) * cw]
        lanes = slice(c * sw, (c + 1) * sw)
        xr = _dot(uc, bbr_ref[c])
        xi = _dot(uc, bbi_ref[c])
        if scan:
            xr = xr.reshape(tt // 8, 8, sw)
            xi = xi.reshape(tt // 8, 8, sw)
            for k in range(3):
                pr = tabr_ref[8 * (k + 1):8 * (k + 2), lanes]
                pi = tabi_ref[8 * (k + 1):8 * (k + 2), lanes]
                sr = pltpu.roll(xr, 2 ** k, 1)
                si = pltpu.roll(xi, 2 ** k, 1)
                xr, xi = xr + (pr * sr - pi * si), xi + (pr * si + pi * sr)
            tr = tabr_ref[0:8, lanes]
            ti = tabi_ref[0:8, lanes]
            hr = cr_s[:, lanes]
            hi = ci_s[:, lanes]
            out_r, out_i = [], []
            for g in range(tt // 8):
                gr = xr[g] + (tr * hr - ti * hi)
                gi = xi[g] + (tr * hi + ti * hr)
                out_r.append(gr)
                out_i.append(gi)
                hr, hi = gr[7:8], gi[7:8]
            xr = jnp.concatenate(out_r, axis=0)
            xi = jnp.concatenate(out_i, axis=0)
            cr_s[:, lanes] = hr
            ci_s[:, lanes] = hi
        else:
            tr = ap2r_ref[0:1, lanes]
            ti = ap2i_ref[0:1, lanes]
            hr = h0r_ref[:, lanes]
            hi = h0i_ref[:, lanes]
            xr, xi = xr + (tr * hr - ti * hi), xi + (tr * hi + ti * hr)
            sr_ref[:, lanes] = xr
            si_ref[:, lanes] = xi
        ys.append(_dot(xr, ccr_ref[c]) - _dot(xi, cci_ref[c]) + d_ref[:, c * cw:(c + 1) * cw] * uc)
    y = _gelu_tanh(jnp.concatenate(ys, axis=-1))
    y_ref[...] = y * _sigmoid(_dot(y, gw_ref[...]) + gb_ref[...])
    if scan:
        sr_ref[0] = cr_s[...]
        si_ref[0] = ci_s[...]


def _s5_specs(j, width, n_chunks, cw, sw, n_state, nidx):
    zero = (0,) * nidx
    return dict(
        ap2=pl.BlockSpec((8, n_state), lambda *a: (0, 0)),
        bb=pl.BlockSpec((n_chunks, cw, sw), lambda *a: (0, 0, 0)),
        cc=pl.BlockSpec((n_chunks, sw, cw), lambda *a: (0, 0, 0)),
        d=pl.BlockSpec((None, 1, width), lambda *a: (j, 0, 0)),
        gw=pl.BlockSpec((None, width, width), lambda *a: (j, 0, 0)),
        gb=pl.BlockSpec((None, 1, width), lambda *a: (j, 0, 0)),
    )


def _s5_seq(proj, row0, bsz, t, tt, j, prm, d3, glu_w, glu_b3):
    ap2r, ap2i, tabr, tabi, bbr, bbi, ccr, cci = prm
    n_chunks, cw, sw = bbr.shape
    width = n_chunks * cw
    n_state = n_chunks * sw
    nt = t // tt
    rb0 = row0 // tt
    sp = _s5_specs(j, width, n_chunks, cw, sw, n_state, 2)
    tab = pl.BlockSpec((S5_TAB_ROWS, n_state), lambda b, i: (0, 0))
    st = pl.BlockSpec((1, 1, n_state), lambda b, i: (b, 0, 0))
    y, sr, si = pl.pallas_call(
        functools.partial(_s5_body, True, n_chunks),
        grid=(bsz, nt),
        in_specs=[pl.BlockSpec((tt, width), lambda b, i: (rb0 + b * nt + i, 0)),
                  sp["ap2"], sp["ap2"], tab, tab, sp["bb"], sp["bb"], sp["cc"], sp["cc"],
                  sp["d"], sp["gw"], sp["gb"]],
        out_specs=[pl.BlockSpec((tt, width), lambda b, i: (b * nt + i, 0)), st, st],
        out_shape=[jax.ShapeDtypeStruct((proj.shape[0], width), F32),
                   jax.ShapeDtypeStruct((bsz, 1, n_state), F32),
                   jax.ShapeDtypeStruct((bsz, 1, n_state), F32)],
        scratch_shapes=[pltpu.VMEM((1, n_state), F32)] * 2,
        compiler_params=_params(2, 48),
        name="s5_seq",
    )(proj, ap2r, ap2i, tabr, tabi, bbr, bbi, ccr, cci, d3, glu_w, glu_b3)
    return y, sr[:, 0], si[:, 0]


def _s5_step(proj, row0, bsz, j, prm, d3, glu_w, glu_b3, h0r, h0i, y_all):
    ap2r, ap2i, _, _, bbr, bbi, ccr, cci = prm
    n_chunks, cw, sw = bbr.shape
    width = n_chunks * cw
    n_state = n_chunks * sw
    rb0 = row0 // bsz
    sp = _s5_specs(j, width, n_chunks, cw, sw, n_state, 1)
    st = pl.BlockSpec((bsz, n_state), lambda i: (0, 0))
    in_specs = [pl.BlockSpec((bsz, width), lambda i: (rb0, 0)),
                sp["ap2"], sp["ap2"], sp["bb"], sp["bb"], sp["cc"], sp["cc"],
                sp["d"], sp["gw"], sp["gb"], st, st]
    args = [proj, ap2r, ap2i, bbr, bbi, ccr, cci, d3, glu_w, glu_b3, h0r, h0i]
    aliases = _shared_out(y_all, in_specs, args, 0)
    return pl.pallas_call(
        functools.partial(_s5_body, False, n_chunks),
        grid=(1,),
        in_specs=in_specs,
        out_specs=[pl.BlockSpec((bsz, width), lambda i: (rb0, 0)), st, st],
        out_shape=[jax.ShapeDtypeStruct(y_all.shape, F32),
                   jax.ShapeDtypeStruct((bsz, n_state), F32),
                   jax.ShapeDtypeStruct((bsz, n_state), F32)],
        input_output_aliases=aliases,
        compiler_params=_params(1, 48),
        name="s5_step",
    )(*args)


def _hgrn_lower_bound(logits, j, axis):
    n = logits.shape[axis]
    mx = jnp.max(logits, axis=axis, keepdims=True)
    e = jnp.exp(logits - mx)
    sm = e / jnp.sum(e, axis=axis, keepdims=True)
    pick = (lambda i: sm[i:i + 1, :]) if axis == 0 else (lambda i: sm[:, i:i + 1])
    cum = pick(0)
    for i in range(1, j + 1):
        cum = cum + pick(i)
    assert j < n
    return cum - pick(0)


def _hgrn_gates(fl, lb):
    logf = _logaddexp(_log_sigmoid(fl), jnp.log(lb) + _log_sigmoid(-fl))
    k = (1.0 - lb) * _sigmoid(-fl)
    return logf, k


def _hgrn_seq_body(j, chunk, sub, q_ref, f_ref, v_ref, gt_ref, lg_ref, gain_ref,
                   y_ref, s_ref, st_s, q_s, k_s, v_s, b_s, o_s):
    tt, dk = q_ref.shape
    i = pl.program_id(2)

    @pl.when(i == 0)
    def _():
        st_s[...] = jnp.zeros_like(st_s)

    lb = _hgrn_lower_bound(lg_ref[...], j, 0)
    logf, k = _hgrn_gates(f_ref[...], lb)
    rin = lax.broadcasted_iota(jnp.int32, (tt, dk), 0) & (chunk - 1)
    b = logf
    s = 1
    while s < chunk:
        b = b + jnp.where(rin >= s, _shift_rows(b, s), 0.0)
        s *= 2
    q_s[...] = _silu(q_ref[...])
    k_s[...] = k
    v_s[...] = v_ref[...]
    b_s[...] = b * LOG2_E
    nsub = chunk // sub
    tsub = lax.broadcasted_iota(jnp.int32, (sub, dk), 0)

    def one_chunk(c, carry):
        r0 = pl.multiple_of(c * chunk, chunk)
        qc = q_s[pl.ds(r0, chunk), :]
        kc = k_s[pl.ds(r0, chunk), :]
        vc = v_s[pl.ds(r0, chunk), :]
        bc = b_s[pl.ds(r0, chunk), :]
        acc = [jnp.zeros((sub, dk), F32) for _ in range(nsub)]
        for s_i in range(chunk):
            ks = kc[s_i:s_i + 1]
            bs = bc[s_i:s_i + 1]
            vs = vc[s_i:s_i + 1]
            sb = s_i // sub
            for tb in range(sb, nsub):
                diff = bc[tb * sub:(tb + 1) * sub] - bs
                if tb == sb:
                    diff = jnp.where(tsub >= s_i - sb * sub, diff, -jnp.inf)
                p = qc[tb * sub:(tb + 1) * sub] * ks * jnp.exp2(diff)
                acc[tb] = acc[tb] + jnp.sum(p, axis=-1, keepdims=True) * vs
        st = st_s[...]
        o = jnp.concatenate(acc, axis=0) + _dot_nt(qc * jnp.exp2(bc), st)
        b_last = bc[chunk - 1:chunk]
        st_s[...] = st * jnp.exp2(b_last) + _dot_tn(vc, kc * jnp.exp2(b_last - bc))
        o_s[pl.ds(r0, chunk), :] = o
        return carry

    lax.fori_loop(0, tt // chunk, one_chunk, 0, unroll=8)
    o = o_s[...]
    o = o * lax.rsqrt(jnp.mean(o * o, axis=-1, keepdims=True) + EPS)
    y_ref[...] = o * gain_ref[...] * _silu(gt_ref[...])

    @pl.when(i == pl.num_programs(2) - 1)
    def _():
        s_ref[...] = st_s[...].T


def _hgrn_seq(proj, row0, bsz, t, tt, j, col0, logits, gain3, chunk=32, sub=8):
    h, dk = HG_HEADS, LANE
    nt = t // tt
    rb0 = row0 // tt
    cb = col0 // dk
    sec = lambda n: pl.BlockSpec((tt, dk), lambda b, hh, i: (rb0 + b * nt + i, cb + n * h + hh))
    y, s = pl.pallas_call(
        functools.partial(_hgrn_seq_body, j, chunk, sub),
        grid=(bsz, h, nt),
        in_specs=[sec(0), sec(1), sec(2), sec(3),
                  pl.BlockSpec((logits.shape[0], dk), lambda b, hh, i: (0, hh)),
                  pl.BlockSpec((None, 1, dk), lambda b, hh, i: (j, 0, hh))],
        out_specs=[pl.BlockSpec((tt, dk), lambda b, hh, i: (b * nt + i, hh)),
                   pl.BlockSpec((None, None, dk, dk), lambda b, hh, i: (b, hh, 0, 0))],
        out_shape=[jax.ShapeDtypeStruct((proj.shape[0], h * dk), F32),
                   jax.ShapeDtypeStruct((bsz, h, dk, dk), F32)],
        scratch_shapes=[pltpu.VMEM((dk, dk), F32)] + [pltpu.VMEM((tt, dk), F32)] * 5,
        compiler_params=_params(3, 32),
        name="hgrn_seq",
    )(proj, proj, proj, proj, logits, gain3)
    return y, s


def _hgrn_step_body(j, bb, q_ref, f_ref, v_ref, gt_ref, qc_ref, fc_ref, lg_ref, lgt_ref, gain_ref,
                    s_ref, *rest):
    y_ref, so_ref = rest[-2:]
    lb_r = _hgrn_lower_bound(lg_ref[...], j, 0)
    lb_c = _hgrn_lower_bound(lgt_ref[...], j, 1)
    qcol = _silu(qc_ref[...])
    logf_c, kcol = _hgrn_gates(fc_ref[...], lb_c)
    fcol = jnp.exp(logf_c)
    for n in range(bb):
        s = s_ref[n]
        q_r = _silu(q_ref[n])
        _, k_r = _hgrn_gates(f_ref[n], lb_r)
        v_r = v_ref[n]
        q_c = qcol[:, n:n + 1]
        f_c = fcol[:, n:n + 1]
        k_c = kcol[:, n:n + 1]
        score = jnp.sum(q_r * k_r, axis=-1, keepdims=True)
        o = score * v_r + jnp.sum((q_c * f_c) * s, axis=0, keepdims=True)
        so_ref[n] = f_c * s + k_c * v_r
        o = o * lax.rsqrt(jnp.mean(o * o, axis=-1, keepdims=True) + EPS)
        y_ref[n:n + 1, :] = o * gain_ref[...] * _silu(gt_ref[n])


def _hgrn_step(proj3, proj_t, j, col0, logits, logits_t, gain3, s_all, s_prev, y_all, row0, bb=8):
    bsz = proj3.shape[0]
    h, dk = HG_HEADS, LANE
    cb = col0 // dk
    qt, ft = proj_t
    row = lambda n: pl.BlockSpec((bb, 1, dk), lambda hh, i: (i, 0, cb + n * h + hh))
    col = pl.BlockSpec((None, None, dk, bb), lambda hh, i: (hh, i, 0, 0))
    st = pl.BlockSpec((None, bb, None, dk, dk), lambda hh, i: (j, i, hh, 0, 0))
    in_specs = [row(0), row(1), row(2), row(3), col, col,
                pl.BlockSpec((logits.shape[0], dk), lambda hh, i: (0, hh)),
                pl.BlockSpec((dk, logits.shape[0]), lambda hh, i: (hh, 0)),
                pl.BlockSpec((None, 1, dk), lambda hh, i: (j, 0, hh)),
                st]
    args = [proj3, proj3, proj3, proj3, qt, ft, logits, logits_t, gain3, s_all]
    aliases = _shared_out(y_all, in_specs, args, 0)
    aliases.update(_shared_out(s_prev, in_specs, args, 1))
    rb0 = row0 // bb
    return pl.pallas_call(
        functools.partial(_hgrn_step_body, j, bb),
        grid=(h, bsz // bb),
        in_specs=in_specs,
        out_specs=[pl.BlockSpec((bb, dk), lambda hh, i: (rb0 + i, hh)), st],
        out_shape=[jax.ShapeDtypeStruct(y_all.shape, F32),
                   jax.ShapeDtypeStruct(s_all.shape, F32)],
        input_output_aliases=aliases,
        compiler_params=_params(2, 32),
        name="hgrn_step",
    )(*args)


def _ml_conv_seq(xm, cw_ref, cb_ref, stage_s, halo):
    tt = xm.shape[0]
    stage_s[0:8, :] = halo
    stage_s[8:, :] = xm
    acc = cb_ref[...] + cw_ref[ML_CONV - 1:ML_CONV, :] * xm
    for d in range(1, ML_CONV):
        acc = acc + cw_ref[ML_CONV - 1 - d:ML_CONV - d, :] * stage_s[8 - d:8 - d + tt, :]
    return _silu(acc)


def _ml_qkv(xc, xm, bdq_ref, bdk_ref, bdv_ref):
    nblk = bdq_ref.shape[0]
    w = bdq_ref.shape[1]
    xcb = xc.astype(BF16)
    xmb = xm.astype(BF16)
    q = jnp.concatenate([_dot(xcb[:, s * w:(s + 1) * w], bdq_ref[s]) for s in range(nblk)], axis=-1)
    k = jnp.concatenate([_dot(xcb[:, s * w:(s + 1) * w], bdk_ref[s]) for s in range(nblk)], axis=-1)
    v = jnp.concatenate([_dot(xmb[:, s * w:(s + 1) * w], bdv_ref[s]) for s in range(nblk)], axis=-1)
    return q, k, v


def _ml_gates_acc(c, q, k, v, wg_ref, bg_ref, g_ref):
    g = _dot(q, wg_ref[0]) + _dot(k, wg_ref[1]) + _dot(v, wg_ref[2])

    @pl.when(c == 0)
    def _():
        g_ref[...] = g + bg_ref[...]

    @pl.when(c > 0)
    def _():
        g_ref[...] += g


def _ml_pre_seq_body(xm_ref, cw_ref, cb_ref, bdq_ref, bdk_ref, bdv_ref, wg_ref, bg_ref,
                     g_ref, stage_s, halo_s):
    i = pl.program_id(1)
    c = pl.program_id(2)
    tt = xm_ref.shape[0]

    @pl.when(i == 0)
    def _():
        halo_s[c] = jnp.zeros(halo_s.shape[1:], F32)

    xm = xm_ref[...]
    xc = _ml_conv_seq(xm, cw_ref, cb_ref, stage_s, halo_s[c])
    halo_s[c] = xm[tt - 8:tt]
    q, k, v = _ml_qkv(xc, xm, bdq_ref, bdk_ref, bdv_ref)
    _ml_gates_acc(c, q, k, v, wg_ref, bg_ref, g_ref)


def _ml_pre_step_body(xm_ref, taps_ref, cw_ref, cb_ref, bdq_ref, bdk_ref, bdv_ref, wg_ref, bg_ref,
                      q_ref, k_ref, v_ref, xc_ref, g_ref):
    c = pl.program_id(0)
    xm = xm_ref[...]
    acc = cb_ref[...] + cw_ref[ML_CONV - 1:ML_CONV, :] * xm
    for d in range(1, ML_CONV):
        acc = acc + cw_ref[ML_CONV - 1 - d:ML_CONV - d, :] * taps_ref[ML_CONV - 1 - d]
    xc = _silu(acc)
    q, k, v = _ml_qkv(xc, xm, bdq_ref, bdk_ref, bdv_ref)
    q_ref[...] = q
    k_ref[...] = k
    v_ref[...] = v
    xc_ref[...] = xc
    _ml_gates_acc(c, q, k, v, wg_ref, bg_ref, g_ref)


def _ml_pre(up, row0, bsz, t, tt, lc, j, conv_w, conv_b3, bd, wg4, bg3, taps=None):
    inner = conv_w.shape[-1]
    ng = wg4.shape[-1]
    nc = inner // lc
    nblk = lc // MXU_DIM
    bdq, bdk, bdv = bd
    seq = taps is None
    rows = bsz * t
    if seq:
        nt = t // tt
        rb0 = row0 // tt
        grid = (bsz, nt, nc)
        rmap = lambda b, i, c: b * nt + i
        cmap = lambda b, i, c: c
    else:
        tt = bsz
        rb0 = row0 // tt
        grid = (nc,)
        rmap = lambda c: 0
        cmap = lambda c: c
    in_specs = [pl.BlockSpec((tt, lc), lambda *a: (rb0 + rmap(*a), cmap(*a)))]
    args = [up]
    if not seq:
        in_specs.append(pl.BlockSpec((ML_CONV - 1, tt, lc), lambda *a: (0, 0, cmap(*a))))
        args.append(taps)
    in_specs += [
        pl.BlockSpec((None, ML_CONV, lc), lambda *a: (j, 0, cmap(*a))),
        pl.BlockSpec((None, 1, lc), lambda *a: (j, 0, cmap(*a))),
        pl.BlockSpec((None, nblk, MXU_DIM, MXU_DIM), lambda *a: (j, cmap(*a), 0, 0)),
        pl.BlockSpec((None, nblk, MXU_DIM, MXU_DIM), lambda *a: (j, cmap(*a), 0, 0)),
        pl.BlockSpec((None, nblk, MXU_DIM, MXU_DIM), lambda *a: (j, cmap(*a), 0, 0)),
        pl.BlockSpec((None, 3, lc, ng), lambda *a: (j, 0, cmap(*a), 0)),
        pl.BlockSpec((None, 1, ng), lambda *a: (j, 0, 0)),
    ]
    args += [conv_w, conv_b3, bdq, bdk, bdv, wg4, bg3]
    big = pl.BlockSpec((tt, lc), lambda *a: (rmap(*a), cmap(*a)))
    n_big = 0 if seq else 4
    return pl.pallas_call(
        _ml_pre_seq_body if seq else _ml_pre_step_body,
        grid=grid,
        in_specs=in_specs,
        out_specs=[big] * n_big + [pl.BlockSpec((tt, ng), lambda *a: (rmap(*a), 0))],
        out_shape=[jax.ShapeDtypeStruct((rows, inner), F32)] * n_big + [jax.ShapeDtypeStruct((rows, ng), F32)],
        scratch_shapes=([pltpu.VMEM((tt + 8, lc), F32), pltpu.VMEM((nc, 8, lc), F32)] if seq else []),
        compiler_params=_params(len(grid), 48),
        name="ml_pre_seq" if seq else "ml_pre_step",
    )(*args)


def _ml_out(h, gain, skip, xc, z):
    mu = jnp.mean(h, axis=-1, keepdims=True)
    hc = h - mu
    hn = hc * lax.rsqrt(jnp.mean(hc * hc, axis=-1, keepdims=True) + EPS)
    return (hn * gain + skip * xc) * _silu(z)


def _ml_seq_body(xm_ref, z_ref, g_ref, gt_ref, cw_ref, cb_ref, bdq_ref, bdk_ref, bdv_ref, gain_ref, skip_ref,
                 o_ref, c_ref, n_ref, m_ref, stage_s, halo_s):
    hh = pl.program_id(1)
    i = pl.program_id(2)
    L, dh = xm_ref.shape
    nh = g_ref.shape[1] // 2

    @pl.when(i == 0)
    def _():
        c_ref[...] = jnp.zeros_like(c_ref)
        n_ref[...] = jnp.zeros_like(n_ref)
        m_ref[...] = jnp.zeros_like(m_ref)
        halo_s[...] = jnp.zeros_like(halo_s)

    xm = xm_ref[...]
    xc = _ml_conv_seq(xm, cw_ref, cb_ref, stage_s, halo_s[...])
    halo_s[...] = xm[L - 8:L]
    q, k, v = _ml_qkv(xc, xm, bdq_ref, bdk_ref, bdv_ref)
    k = k * (dh ** -0.5)

    g = g_ref[...]
    gt = gt_ref[...]
    lane = lax.broadcasted_iota(jnp.int32, g.shape, 1)
    sub = lax.broadcasted_iota(jnp.int32, gt.shape, 0)
    ig_c = jnp.sum(jnp.where(lane == hh, g, 0.0), axis=1, keepdims=True)
    lf_c = _log_sigmoid(jnp.sum(jnp.where(lane == nh + hh, g, 0.0), axis=1, keepdims=True))
    ig_r = jnp.sum(jnp.where(sub == hh, gt, 0.0), axis=0, keepdims=True)
    lf_r = _log_sigmoid(jnp.sum(jnp.where(sub == nh + hh, gt, 0.0), axis=0, keepdims=True))
    tr = lax.broadcasted_iota(jnp.int32, (L, L), 0)
    tc = lax.broadcasted_iota(jnp.int32, (L, L), 1)
    causal = tr >= tc
    b_c = jnp.sum(jnp.where(causal, lf_r, 0.0), axis=1, keepdims=True)
    b_r = jnp.sum(jnp.where(tr <= tc, lf_c, 0.0), axis=0, keepdims=True)
    m = m_ref[...]
    dlog = jnp.where(causal, b_c - b_r + ig_r, -jnp.inf)
    gsum = b_c + m
    mt = jnp.maximum(gsum, jnp.max(dlog, axis=1, keepdims=True))
    w = jnp.exp(dlog - mt)
    gi = jnp.exp(gsum - mt)
    c_old = c_ref[...]
    n_old = n_ref[...]
    qk = _dot_nt(q, k) * w
    num = _dot(qk, v) + gi * _dot(q, c_old)
    den = jnp.sum(qk, axis=1, keepdims=True) + gi * jnp.sum(q * n_old, axis=1, keepdims=True)
    h = num / jnp.maximum(jnp.abs(den), jnp.exp(-mt))
    m_new = mt[L - 1:L]
    b_last = b_c[L - 1:L]
    decay = jnp.exp(b_last + m - m_new)
    ks = k * jnp.exp(b_last - b_c + ig_c - m_new)
    c_ref[...] = decay * c_old + _dot_tn(ks, v)
    n_ref[...] = decay * n_old + jnp.sum(ks, axis=0, keepdims=True)
    m_ref[...] = m_new
    o_ref[...] = _ml_out(h, gain_ref[...], skip_ref[...], xc, z_ref[...])


def _ml_seq(up, row0, gates, gates_t, bsz, t, L, j, gain3, skip3, conv_w, conv_b3, bd):
    inner = conv_w.shape[-1]
    nh = ML_HEADS
    dh = inner // nh
    nt = t // L
    rb0 = row0 // L
    nblk = dh // MXU_DIM
    blk = pl.BlockSpec((L, dh), lambda b, hh, i: (b * nt + i, hh))
    par = pl.BlockSpec((None, 1, dh), lambda b, hh, i: (j, 0, hh))
    tile = pl.BlockSpec((None, nblk, MXU_DIM, MXU_DIM), lambda b, hh, i: (0, hh, 0, 0))
    bdq, bdk, bdv = bd
    return pl.pallas_call(
        _ml_seq_body,
        grid=(bsz, nh, nt),
        in_specs=[pl.BlockSpec((L, dh), lambda b, hh, i: (rb0 + b * nt + i, hh)),
                  pl.BlockSpec((L, dh), lambda b, hh, i: (rb0 + b * nt + i, nh + hh)),
                  pl.BlockSpec((L, 2 * nh), lambda b, hh, i: (b * nt + i, 0)),
                  pl.BlockSpec((2 * nh, L), lambda b, hh, i: (0, b * nt + i)),
                  pl.BlockSpec((None, ML_CONV, dh), lambda b, hh, i: (0, 0, hh)),
                  pl.BlockSpec((None, 1, dh), lambda b, hh, i: (0, 0, hh)),
                  tile, tile, tile, par, par],
        out_specs=[blk,
                   pl.BlockSpec((None, None, dh, dh), lambda b, hh, i: (b, hh, 0, 0)),
                   pl.BlockSpec((None, None, 1, dh), lambda b, hh, i: (b, hh, 0, 0)),
                   pl.BlockSpec((None, None, 1, 1), lambda b, hh, i: (b, hh, 0, 0))],
        out_shape=[jax.ShapeDtypeStruct((up.shape[0], inner), F32),
                   jax.ShapeDtypeStruct((bsz, nh, dh, dh), F32),
                   jax.ShapeDtypeStruct((bsz, nh, 1, dh), F32),
                   jax.ShapeDtypeStruct((bsz, nh, 1, 1), F32)],
        scratch_shapes=[pltpu.VMEM((L + 8, dh), F32), pltpu.VMEM((8, dh), F32)],
        compiler_params=_params(3, 48),
        name="ml_seq",
    )(up, up, gates, gates_t, conv_w, conv_b3, bdq, bdk, bdv, gain3, skip3)


def _ml_step_body(bb, q_ref, k_ref, v_ref, xc_ref, z_ref, qc_ref, kc_ref, ig_ref, lf_ref, m_ref,
                  gain_ref, skip_ref, c_ref, n_ref, *rest):
    o_ref, co_ref, no_ref, mo_ref = rest[-4:]
    dh = q_ref.shape[-1]
    scale = dh ** -0.5
    ig_all = ig_ref[...]
    lf_all = _log_sigmoid(lf_ref[...])
    m_all = m_ref[...]
    g_all = lf_all + m_all
    mt_all = jnp.maximum(g_all, ig_all)
    w_all = jnp.exp(ig_all - mt_all)
    gi_all = jnp.exp(g_all - mt_all)
    mo_ref[...] = mt_all
    for n in range(bb):
        c_old = c_ref[n]
        n_old = n_ref[n]
        q_r = q_ref[n]
        k_r = k_ref[n] * scale
        v_r = v_ref[n]
        q_c = qc_ref[:, n:n + 1]
        k_c = kc_ref[:, n:n + 1] * scale
        mt = mt_all[:, n:n + 1]
        w = w_all[:, n:n + 1]
        gi = gi_all[:, n:n + 1]
        qk = jnp.sum(q_r * k_r, axis=-1, keepdims=True) * w
        num = qk * v_r + gi * jnp.sum(q_c * c_old, axis=0, keepdims=True)
        den = qk + gi * jnp.sum(q_r * n_old, axis=-1, keepdims=True)
        h = num / jnp.maximum(jnp.abs(den), jnp.exp(-mt))
        co_ref[n] = gi * c_old + (w * k_c) * v_r
        no_ref[n] = gi * n_old + w * k_r
        o_ref[n:n + 1, :] = _ml_out(h, gain_ref[...], skip_ref[...], xc_ref[n], z_ref[n])


def _ml_step(q3, k3, v3, xc3, up3, q_t, k_t, ig_t, lf_t, m_t, j, gain3, skip3, c_all, n_all, c_prev,
             y_all, row0, bb=8):
    bsz, _, inner = q3.shape
    nh = ML_HEADS
    dh = inner // nh
    row = pl.BlockSpec((bb, 1, dh), lambda hh, i: (i, 0, hh))
    col = pl.BlockSpec((None, None, dh, bb), lambda hh, i: (hh, i, 0, 0))
    sc = pl.BlockSpec((None, None, 1, bb), lambda hh, i: (hh, i, 0, 0))
    par = pl.BlockSpec((None, 1, dh), lambda hh, i: (j, 0, hh))
    nst = pl.BlockSpec((bb, None, 1, dh), lambda hh, i: (i, hh, 0, 0))
    cst = pl.BlockSpec((None, bb, None, dh, dh), lambda hh, i: (j, i, hh, 0, 0))
    nst_in = pl.BlockSpec((None, bb, None, 1, dh), lambda hh, i: (j, i, hh, 0, 0))
    in_specs = [row, row, row, row,
                pl.BlockSpec((bb, 1, dh), lambda hh, i: (i, 0, nh + hh)),
                col, col, sc, sc, sc, par, par, cst, nst_in]
    args = [q3, k3, v3, xc3, up3, q_t, k_t, ig_t, lf_t, m_t, gain3, skip3, c_all, n_all]
    aliases = _shared_out(y_all, in_specs, args, 0)
    aliases.update(_shared_out(c_prev, in_specs, args, 1))
    rb0 = row0 // bb
    return pl.pallas_call(
        functools.partial(_ml_step_body, bb),
        grid=(nh, bsz // bb),
        in_specs=in_specs,
        out_specs=[pl.BlockSpec((bb, dh), lambda hh, i: (rb0 + i, hh)), cst, nst, sc],
        out_shape=[jax.ShapeDtypeStruct(y_all.shape, F32),
                   jax.ShapeDtypeStruct(c_all.shape, F32),
                   jax.ShapeDtypeStruct(n_all.shape[1:], F32),
                   jax.ShapeDtypeStruct(m_t.shape, F32)],
        input_output_aliases=aliases,
        compiler_params=_params(2, 56),
        name="ml_step",
    )(*args)


def _block_diag(w, size):
    n, c, _ = w.shape
    per = size // c
    w4 = w.reshape(n // per, per, c, c)
    eye = jnp.eye(per, dtype=w.dtype)
    out = w4[:, :, :, None, :] * eye[None, :, None, :, None]
    return out.reshape(n // per, size, size).astype(BF16)


def _to_cols(x, nh, bb):
    bsz, w = x.shape
    d = w // nh
    return x.reshape(bsz // bb, bb, nh, d).transpose(2, 0, 3, 1)


def _s5_layout(bbt_r, bbt_i, c_re, c_im, n_chunks):
    ch, n_state = bbt_r.shape
    g = c_re.shape[0]
    p = n_state // g
    per = g // n_chunks
    eye = jnp.eye(per, dtype=F32)

    def bmat(bt):
        b4 = bt.reshape(ch, n_chunks, per, p).transpose(1, 0, 2, 3)
        out = eye[None, :, None, :, None] * b4[:, None, :, :, :]
        return out.reshape(n_chunks, per * ch, per * p).astype(BF16)

    def cmat(c):
        c4 = c.reshape(n_chunks, per, ch, p).transpose(0, 1, 3, 2)
        out = c4[:, :, :, None, :] * eye[None, :, None, :, None]
        return out.reshape(n_chunks, per * p, per * ch).astype(BF16)

    return bmat(bbt_r), bmat(bbt_i), cmat(c_re), cmat(c_im)


TM_FFN = 1040
TF_FFN = 256
TM_PROJ = 2080
TN_PROJ = 512
TM_OUT = 1040
TK_OUT = 512
TT_S5 = 512
TT_HGRN = 1024
TT_MLPRE = 1024
LC_MLPRE = 512
L_ML = 256
S5_CHUNKS = 8


def kernel(x_prompt, x_sample, state_s5_re, state_s5_im, state_hgrn, state_conv, state_mlstm_c, state_mlstm_n, state_mlstm_m, norm_gain, ffn_w_gate, ffn_w_up, ffn_w_down, ab_w_in, ab_w_out, s5_lambda_re, s5_lambda_im, s5_log_step, s5_b_re, s5_b_im, s5_c_re, s5_c_im, s5_d, s5_glu_w, s5_glu_b, hgrn_lb_logits, hgrn_norm_gain, ml_w_up, ml_conv_w, ml_conv_b, ml_w_q, ml_w_k, ml_w_v, ml_w_gates, ml_b_gates, ml_norm_gain, ml_skip, ml_w_down):
    bp, t, d = x_prompt.shape
    bs = x_sample.shape[0]
    depth = norm_gain.shape[0]
    mp = bp * t
    n_ab, g5, p5 = s5_lambda_re.shape
    n_state = g5 * p5
    s5_w = g5 * S5_GROUP
    hg_w = HG_HEADS * LANE
    inner = ml_conv_w.shape[-1]
    n_c = ml_conv_w.shape[0]
    dh = inner // ML_HEADS

    gains = norm_gain.reshape(depth, norm_gain.shape[1], 1, d)
    h = jnp.concatenate([x_prompt.reshape(mp, d), x_sample.reshape(bs, d)], axis=0)

    out = dict(p_s5r=[], p_s5i=[], p_hg=[], p_conv=[], p_c=[], p_n=[], p_m=[],
               s_s5r=[], s_s5i=[], s_conv=[], s_n=[], s_m=[])
    s_hg = s_c = None

    for layer in range(depth):
        j = layer // 2
        h = _ffn(h, gains, ffn_w_gate, ffn_w_up, ffn_w_down, layer, 0, 0, 1, TM_FFN, TF_FFN)
        if layer % 2 == 0:
            proj = _norm_matmul(h, gains, layer, 2, ab_w_in, j, TM_PROJ, TN_PROJ)
            prm = _s5_params(s5_lambda_re[j].reshape(1, n_state), s5_lambda_im[j].reshape(1, n_state),
                             jnp.repeat(s5_log_step[j], p5).reshape(1, n_state),
                             s5_b_re[j].transpose(2, 0, 1).reshape(S5_GROUP, n_state),
                             s5_b_im[j].transpose(2, 0, 1).reshape(S5_GROUP, n_state))
            ap2r, ap2i, tabr, tabi, bbt_r, bbt_i = prm
            bbr, bbi, ccr, cci = _s5_layout(bbt_r, bbt_i, s5_c_re[j], s5_c_im[j], S5_CHUNKS)
            s5p = (ap2r, ap2i, tabr, tabi, bbr, bbi, ccr, cci)
            d3 = s5_d.reshape(n_ab, 1, s5_w)
            gb3 = s5_glu_b.reshape(n_ab, 1, s5_w)
            ya, sr_p, si_p = _s5_seq(proj, 0, bp, t, TT_S5, j, s5p, d3, s5_glu_w, gb3)
            ya, sr_s, si_s = _s5_step(proj, mp, bs, j, s5p, d3, s5_glu_w, gb3,
                                      state_s5_re[j].reshape(bs, n_state), state_s5_im[j].reshape(bs, n_state), ya)
            out["p_s5r"].append(sr_p.reshape(bp, g5, p5))
            out["p_s5i"].append(si_p.reshape(bp, g5, p5))
            out["s_s5r"].append(sr_s.reshape(bs, g5, p5))
            out["s_s5i"].append(si_s.reshape(bs, g5, p5))
            hgain3 = hgrn_norm_gain.reshape(n_ab, 1, hg_w)
            yb, hg_p = _hgrn_seq(proj, 0, bp, t, TT_HGRN, j, s5_w, hgrn_lb_logits, hgain3)
            proj_s = proj[mp:]
            bb_h = 16
            yb, s_hg = _hgrn_step(
                proj_s.reshape(bs, 1, proj.shape[1]),
                (_to_cols(proj_s[:, s5_w:s5_w + hg_w], HG_HEADS, bb_h),
                 _to_cols(proj_s[:, s5_w + hg_w:s5_w + 2 * hg_w], HG_HEADS, bb_h)),
                j, s5_w, hgrn_lb_logits, hgrn_lb_logits.T, hgain3, state_hgrn, s_hg, yb, mp, bb_h)
            out["p_hg"].append(hg_p)
            h = _matmul_postnorm([ya, yb], ab_w_out, j, h, gains, layer, 3, TM_OUT, TK_OUT)
        else:
            up = _norm_matmul(h, gains, layer, 2, ml_w_up, j, TM_PROJ, TN_PROJ)
            bd = (_block_diag(ml_w_q[j], MXU_DIM)[None], _block_diag(ml_w_k[j], MXU_DIM)[None],
                  _block_diag(ml_w_v[j], MXU_DIM)[None])
            wg4 = ml_w_gates[j].reshape(1, 3, inner, 2 * ML_HEADS)
            bg3 = ml_b_gates[j].reshape(1, 1, 2 * ML_HEADS)
            cw = ml_conv_w[j][None]
            cb3 = ml_conv_b[j].reshape(1, 1, inner)
            gain3 = ml_norm_gain.reshape(n_c, 1, inner)
            skip3 = ml_skip.reshape(n_c, 1, inner)
            (g_p,) = _ml_pre(up, 0, bp, t, TT_MLPRE, LC_MLPRE, 0, cw, cb3, bd, wg4, bg3)
            hf, c_p, n_p, m_p = _ml_seq(up, 0, g_p, g_p.T, bp, t, L_ML, j, gain3, skip3, cw, cb3, bd)
            out["p_conv"].append(jnp.stack([
                lax.slice(up, ((b + 1) * t - (ML_CONV - 1), 0), ((b + 1) * t, inner)) for b in range(bp)]))
            out["p_c"].append(c_p)
            out["p_n"].append(n_p.reshape(bp, ML_HEADS, dh))
            out["p_m"].append(m_p.reshape(bp, ML_HEADS))
            taps = state_conv[j].transpose(1, 0, 2)
            q_s, k_s, v_s, xc_s, g_s = _ml_pre(up, mp, bs, 1, None, LC_MLPRE, 0, cw, cb3, bd, wg4, bg3, taps=taps)
            bb_m = 8
            sc = lambda a: a.T.reshape(ML_HEADS, bs // bb_m, 1, bb_m)
            r3 = lambda a: a.reshape(bs, 1, a.shape[-1])
            hf, s_c, n_s, m_s = _ml_step(
                r3(q_s), r3(k_s), r3(v_s), r3(xc_s), r3(up[mp:]),
                _to_cols(q_s, ML_HEADS, bb_m), _to_cols(k_s, ML_HEADS, bb_m),
                sc(g_s[:, :ML_HEADS]), sc(g_s[:, ML_HEADS:]), sc(state_mlstm_m[j]),
                j, gain3, skip3, state_mlstm_c, state_mlstm_n.reshape(n_c, bs, ML_HEADS, 1, dh), s_c,
                hf, mp, bb_m)
            out["s_conv"].append(jnp.concatenate([state_conv[j][:, 1:], up[mp:, None, :inner]], axis=1))
            out["s_n"].append(n_s.reshape(bs, ML_HEADS, dh))
            out["s_m"].append(m_s.reshape(ML_HEADS, bs).T)
            h = _matmul_postnorm([hf], ml_w_down, j, h, gains, layer, 3, TM_OUT, TK_OUT)
        h = _ffn(h, gains, ffn_w_gate, ffn_w_up, ffn_w_down, layer, 1, 4, 5, TM_FFN, TF_FFN)

    st = lambda name: jnp.stack(out[name])
    return (h[:mp].reshape(bp, t, d), h[mp:].reshape(bs, 1, d),
            st("p_s5r"), st("p_s5i"), st("p_hg"), st("p_conv"), st("p_c"), st("p_n"), st("p_m"),
            st("s_s5r"), st("s_s5i"), s_hg, st("s_conv"), s_c, st("s_n"), st("s_m"))
```
